```python
import math
import jax, jax.numpy as jnp
from jax import lax
import numpy as np

D_MODEL = 2048
BATCH = 2
SEQ = 8192
DEPTH = 1

CHUNK = 64
D_MIX = D_MODEL
RET_WIDTH = D_MIX // 2
RET_HEAD_DIM = 256
RET_HEADS = RET_WIDTH // RET_HEAD_DIM
SSD_WIDTH = D_MIX - RET_WIDTH
SSD_HEAD_DIM = 64
SSD_HEADS = SSD_WIDTH // SSD_HEAD_DIM
SSD_GROUPS = 2
SSD_HEADS_PER_GROUP = SSD_HEADS // SSD_GROUPS
SSD_STATE = 128
CONV_WIDTH = 4
CONV_CH = SSD_WIDTH + 2 * SSD_GROUPS * SSD_STATE
D_IN_PROJ = 4 * RET_WIDTH + SSD_WIDTH + CONV_CH + SSD_HEADS
D_FF = ((8 * D_MODEL // 3 + 255) // 256) * 256
ROPE_BASE = 10000.0
LN_EPS = 1e-5
DEEPNORM_ALPHA = (2.0 * DEPTH) ** 0.25
DEEPNORM_BETA = (8.0 * DEPTH) ** -0.25
FFN_RES_WEIGHT = 0.5

kernel_name = "hymba_retention_ssd_macaron_deepnorm"


def layer_norm(x, gain, bias):
    xf = x.astype(jnp.float32)
    mu = jnp.mean(xf, axis=-1, keepdims=True)
    var = jnp.mean(jnp.square(xf - mu), axis=-1, keepdims=True)
    return ((xf - mu) * lax.rsqrt(var + LN_EPS)).astype(x.dtype) * gain + bias


def swiglu_ffn(x, w_gate, w_up, w_down):
    return (jax.nn.silu(x @ w_gate) * (x @ w_up)) @ w_down


def rotate_every_two(t, positions):
    half = t.shape[-1] // 2
    inv_freq = 1.0 / (ROPE_BASE ** jnp.linspace(0.0, 1.0, half, dtype=jnp.float32))
    theta = positions.astype(jnp.float32)[..., None] * inv_freq
    cos = jnp.cos(theta)[:, :, None, :].astype(t.dtype)
    sin = jnp.sin(theta)[:, :, None, :].astype(t.dtype)
    t2 = t.reshape(t.shape[:-1] + (half, 2))
    t_even, t_odd = t2[..., 0], t2[..., 1]
    out = jnp.stack([t_even * cos - t_odd * sin, t_odd * cos + t_even * sin], axis=-1)
    return out.reshape(t.shape)


def multiscale_retention(q, k, v, positions):
    b, l, _ = q.shape
    nc = l // CHUNK
    dtype = q.dtype
    shp = (b, l, RET_HEADS, RET_HEAD_DIM)
    q = rotate_every_two(q.reshape(shp), positions)
    k = rotate_every_two(k.reshape(shp), positions) * (RET_HEAD_DIM ** -0.5)
    v = v.reshape(shp)
    log_gamma = jnp.log(1.0 - 2.0 ** (-5.0 - jnp.arange(RET_HEADS, dtype=jnp.float32)))
    pos = jnp.arange(CHUNK, dtype=jnp.float32)
    rel = pos[:, None] - pos[None, :]
    intra_decay = jnp.where(rel >= 0, jnp.exp(log_gamma[:, None, None] * jnp.maximum(rel, 0.0)), 0.0).astype(dtype)
    query_decay = jnp.exp(log_gamma[None, :] * (pos[:, None] + 1.0)).astype(dtype)
    key_decay = jnp.exp(log_gamma[None, :] * (CHUNK - 1.0 - pos[:, None])).astype(dtype)
    chunk_decay = jnp.exp(log_gamma * CHUNK).astype(dtype)
    cshp = (b, nc, CHUNK, RET_HEADS, RET_HEAD_DIM)
    qc, kc, vc = q.reshape(cshp), k.reshape(cshp), v.reshape(cshp)
    scores = jnp.einsum('bzihd,bzjhd->bzhij', qc, kc) * intra_decay
    o_intra = jnp.einsum('bzhij,bzjhe->bzihe', scores, vc)

    def step(state, inp):
        q_z, k_z, v_z = inp
        o_z = jnp.einsum('bihd,bhde->bihe', q_z, state) * query_decay[None, :, :, None]
        state = state * chunk_decay[None, :, None, None] + jnp.einsum('bjhd,bjhe,jh->bhde', k_z, v_z, key_decay)
        return state, o_z

    state0 = jnp.zeros((b, RET_HEADS, RET_HEAD_DIM, RET_HEAD_DIM), dtype)
    _, o_inter = lax.scan(step, state0, (jnp.moveaxis(qc, 1, 0), jnp.moveaxis(kc, 1, 0), jnp.moveaxis(vc, 1, 0)))
    o = o_intra + jnp.moveaxis(o_inter, 0, 1)
    return o.reshape(shp)


def head_group_norm(o, gain, bias):
    of = o.astype(jnp.float32)
    mu = jnp.mean(of, axis=-1, keepdims=True)
    var = jnp.mean(jnp.square(of - mu), axis=-1, keepdims=True)
    on = ((of - mu) * lax.rsqrt(var + LN_EPS)).astype(o.dtype)
    b, l = o.shape[0], o.shape[1]
    return on.reshape(b, l, RET_WIDTH) * gain + bias


def ssd_mixer(z, xbc, dt_raw, conv_w, conv_b, dt_bias, a_log, d_skip, norm_gain):
    b, l, _ = z.shape
    nc = l // CHUNK
    G, R, P, N = SSD_GROUPS, SSD_HEADS_PER_GROUP, SSD_HEAD_DIM, SSD_STATE
    xpad = jnp.pad(xbc, ((0, 0), (CONV_WIDTH - 1, 0), (0, 0)))
    conv = conv_b + sum(xpad[:, t:t + l] * conv_w[t] for t in range(CONV_WIDTH))
    xbc = jax.nn.silu(conv)
    xs, bm, cm = jnp.split(xbc, [SSD_WIDTH, SSD_WIDTH + G * N], axis=-1)
    x = xs.reshape(b, nc, CHUNK, G, R, P)
    bm = bm.reshape(b, nc, CHUNK, G, N)
    cm = cm.reshape(b, nc, CHUNK, G, N)
    dt = jax.nn.softplus((dt_raw + dt_bias).astype(jnp.float32)).reshape(b, nc, CHUNK, G, R)
    a = -jnp.exp(a_log.astype(jnp.float32)).reshape(G, R)
    a_cum = jnp.cumsum(dt * a, axis=2)
    xdt = x * dt[..., None].astype(x.dtype)
    seg = a_cum[:, :, :, None] - a_cum[:, :, None, :]
    idx = jnp.arange(CHUNK)
    causal = (idx[:, None] >= idx[None, :])[None, None, :, :, None, None]
    decay = jnp.exp(jnp.where(causal, seg, -jnp.inf)).astype(x.dtype)
    cb = jnp.einsum('bzign,bzjgn->bzijg', cm, bm)
    y_intra = jnp.einsum('bzijgr,bzjgrp->bzigrp', cb[..., None] * decay, xdt)

    def step(state, inp):
        c_z, b_z, xdt_z, acum_z = inp
        from_start = jnp.exp(acum_z).astype(state.dtype)
        to_end = jnp.exp(acum_z[:, -1:] - acum_z).astype(state.dtype)
        total = jnp.exp(acum_z[:, -1]).astype(state.dtype)
        y_z = jnp.einsum('bign,bgrpn->bigrp', c_z, state) * from_start[..., None]
        state = state * total[..., None, None] + jnp.einsum('bjgn,bjgr,bjgrp->bgrpn', b_z, to_end, xdt_z)
        return state, y_z

    state0 = jnp.zeros((b, G, R, P, N), x.dtype)
    _, y_inter = lax.scan(step, state0, (jnp.moveaxis(cm, 1, 0), jnp.moveaxis(bm, 1, 0),
                                         jnp.moveaxis(xdt, 1, 0), jnp.moveaxis(a_cum, 1, 0)))
    y = y_intra + jnp.moveaxis(y_inter, 0, 1) + x * d_skip.reshape(G, R)[:, :, None]
    y = y.reshape(b, l, SSD_WIDTH) * jax.nn.silu(z)
    yg = y.reshape(b, l, G, SSD_WIDTH // G).astype(jnp.float32)
    yg = yg * lax.rsqrt(jnp.mean(jnp.square(yg), axis=-1, keepdims=True) + LN_EPS)
    return yg.reshape(b, l, SSD_WIDTH).astype(z.dtype) * norm_gain


def hybrid_mixer(h, positions, w_in, ret_gn_gain, ret_gn_bias, conv_w, conv_b,
                 dt_bias, a_log, d_skip, ssd_norm_gain, w_out):
    proj = h @ w_in
    splits = [RET_WIDTH, 2 * RET_WIDTH, 3 * RET_WIDTH, 4 * RET_WIDTH,
              4 * RET_WIDTH + SSD_WIDTH, 4 * RET_WIDTH + SSD_WIDTH + CONV_CH]
    q, k, v, g, z, xbc, dt_raw = jnp.split(proj, splits, axis=-1)
    ret = multiscale_retention(q, k, v, positions)
    ret_out = jax.nn.silu(g) * head_group_norm(ret, ret_gn_gain, ret_gn_bias)
    ssd_out = ssd_mixer(z, xbc, dt_raw, conv_w, conv_b, dt_bias, a_log, d_skip, ssd_norm_gain)
    return jnp.concatenate([ret_out, ssd_out], axis=-1) @ w_out


def setup_inputs(seed: int = 0) -> dict:
    key = jax.random.key(seed)
    ks = jax.random.split(key, 26)
    f32 = jnp.float32

    def normal(k, shape, scale):
        return jax.random.normal(k, shape, f32) * scale

    x = jax.random.normal(ks[0], (BATCH, SEQ, D_MODEL), f32)
    offset = jax.random.randint(ks[1], (BATCH, 1), 0, 1024, dtype=jnp.int32) * CHUNK
    positions = (offset + jnp.arange(SEQ, dtype=jnp.int32)[None, :]).astype(jnp.int32)
    dt_init = jnp.exp(jax.random.uniform(ks[13], (DEPTH, SSD_HEADS), f32, math.log(1e-3), math.log(1e-1)))
    ssd_dt_bias = dt_init + jnp.log(-jnp.expm1(-dt_init))
    ssd_a_log = jnp.log(jax.random.uniform(ks[14], (DEPTH, SSD_HEADS), f32, 1.0, 16.0))
    return {
        "x": x,
        "positions": positions,
        "ffn1_w_gate": normal(ks[2], (DEPTH, D_MODEL, D_FF), D_MODEL ** -0.5),
        "ffn1_w_up": normal(ks[3], (DEPTH, D_MODEL, D_FF), D_MODEL ** -0.5),
        "ffn1_w_down": normal(ks[4], (DEPTH, D_FF, D_MODEL), DEEPNORM_BETA * D_FF ** -0.5),
        "ln1_gain": 1.0 + normal(ks[5], (DEPTH, D_MODEL), 0.02),
        "ln1_bias": normal(ks[6], (DEPTH, D_MODEL), 0.02),
        "mix_w_in": normal(ks[7], (DEPTH, D_MODEL, D_IN_PROJ), D_MODEL ** -0.5),
        "ret_gn_gain": 1.0 + normal(ks[8], (DEPTH, RET_WIDTH), 0.02),
        "ret_gn_bias": normal(ks[9], (DEPTH, RET_WIDTH), 0.02),
        "ssd_conv_w": normal(ks[10], (DEPTH, CONV_WIDTH, CONV_CH), CONV_WIDTH ** -0.5),
        "ssd_conv_b": normal(ks[11], (DEPTH, CONV_CH), 0.02),
        "ssd_dt_bias": ssd_dt_bias,
        "ssd_a_log": ssd_a_log,
        "ssd_d": 1.0 + normal(ks[15], (DEPTH, SSD_HEADS), 0.02),
        "ssd_norm_gain": 1.0 + normal(ks[16], (DEPTH, SSD_WIDTH), 0.02),
        "mix_w_out": normal(ks[17], (DEPTH, D_MIX, D_MODEL), DEEPNORM_BETA * D_MIX ** -0.5),
        "ln2_gain": 1.0 + normal(ks[18], (DEPTH, D_MODEL), 0.02),
        "ln2_bias": normal(ks[19], (DEPTH, D_MODEL), 0.02),
        "ffn2_w_gate": normal(ks[20], (DEPTH, D_MODEL, D_FF), D_MODEL ** -0.5),
        "ffn2_w_up": normal(ks[21], (DEPTH, D_MODEL, D_FF), D_MODEL ** -0.5),
        "ffn2_w_down": normal(ks[22], (DEPTH, D_FF, D_MODEL), DEEPNORM_BETA * D_FF ** -0.5),
        "ln3_gain": 1.0 + normal(ks[23], (DEPTH, D_MODEL), 0.02),
        "ln3_bias": normal(ks[24], (DEPTH, D_MODEL), 0.02),
    }


def reference(x, positions, ffn1_w_gate, ffn1_w_up, ffn1_w_down, ln1_gain, ln1_bias,
              mix_w_in, ret_gn_gain, ret_gn_bias, ssd_conv_w, ssd_conv_b, ssd_dt_bias,
              ssd_a_log, ssd_d, ssd_norm_gain, mix_w_out, ln2_gain, ln2_bias,
              ffn2_w_gate, ffn2_w_up, ffn2_w_down, ln3_gain, ln3_bias):
    for layer in range(DEPTH):
        x = layer_norm(DEEPNORM_ALPHA * x + FFN_RES_WEIGHT * swiglu_ffn(x, ffn1_w_gate[layer], ffn1_w_up[layer], ffn1_w_down[layer]),
                       ln1_gain[layer], ln1_bias[layer])
        mix = hybrid_mixer(x, positions, mix_w_in[layer], ret_gn_gain[layer], ret_gn_bias[layer],
                           ssd_conv_w[layer], ssd_conv_b[layer], ssd_dt_bias[layer], ssd_a_log[layer],
                           ssd_d[layer], ssd_norm_gain[layer], mix_w_out[layer])
        x = layer_norm(DEEPNORM_ALPHA * x + mix, ln2_gain[layer], ln2_bias[layer])
        x = layer_norm(DEEPNORM_ALPHA * x + FFN_RES_WEIGHT * swiglu_ffn(x, ffn2_w_gate[layer], ffn2_w_up[layer], ffn2_w_down[layer]),
                       ln3_gain[layer], ln3_bias[layer])
    return x
```

```python
import functools
import math

import jax
import jax.numpy as jnp
import numpy as np
from jax import lax
from jax.experimental import pallas as pl
from jax.experimental.pallas import tpu as pltpu

F32 = jnp.float32
BF16 = jnp.bfloat16

D_MODEL = 2048
RET_WIDTH = 1024
RET_HEAD_DIM = 256
RET_HEADS = 4
SSD_WIDTH = 1024
SSD_HEAD_DIM = 64
SSD_HEADS = 16
SSD_GROUPS = 2
SSD_STATE = 128
CONV_WIDTH = 4
CONV_CH = SSD_WIDTH + 2 * SSD_GROUPS * SSD_STATE
D_FF = 5632
ROPE_BASE = 10000.0
LN_EPS = 1e-5
FFN_RES_WEIGHT = 0.5

LANES = 128
SUBLANES = 8
VMEM_BYTES_V7X = 64 * 1024 * 1024
VMEM_LIMIT = VMEM_BYTES_V7X - 8 * 1024 * 1024

FFN_TM = 512
FFN_TF = 512
PROJ_TM = 1024
PROJ_TN = 512
OUT_TM = 512
RET_CHUNK = 256
SSD_CHUNK = 128
SSD_STEP = 256
ROPE_TM = 1024


def _cparams(sem):
    return pltpu.CompilerParams(dimension_semantics=sem, vmem_limit_bytes=VMEM_LIMIT)


def _silu(v):
    return v / (1.0 + jnp.exp(-v))


def _layer_norm_rows(y, gain, bias):
    mu = jnp.mean(y, axis=-1, keepdims=True)
    d = y - mu
    var = jnp.mean(d * d, axis=-1, keepdims=True)
    return d * lax.rsqrt(var + LN_EPS) * gain + bias


def _split3(v):
    h0 = v.astype(BF16)
    r1 = v - h0.astype(F32)
    h1 = r1.astype(BF16)
    h2 = (r1 - h1.astype(F32)).astype(BF16)
    return h0, h1, h2


def _dot(a, b):
    return jnp.dot(a, b, preferred_element_type=F32)


def _dot_exact01(m01, v, left=True):
    parts = _split3(v)
    if left:
        return _dot(m01, parts[0]) + _dot(m01, parts[1]) + _dot(m01, parts[2])
    return _dot(parts[0], m01) + _dot(parts[1], m01) + _dot(parts[2], m01)


def _ffn_ln_kernel(x_ref, wgu_ref, wd_ref, gain_ref, bias_ref, o_ref, ob_ref, xb_ref, *, alpha, n_f):
    f = pl.program_id(1)

    @pl.when(f == 0)
    def _():
        xb_ref[...] = x_ref[...].astype(BF16)

    gu = _dot(xb_ref[...], wgu_ref[...])
    tf = gu.shape[1] // 2
    h = (_silu(gu[:, :tf]) * gu[:, tf:]).astype(BF16)
    part = _dot(h, wd_ref[...])

    @pl.when(f == 0)
    def _():
        o_ref[...] = part

    @pl.when(f > 0)
    def _():
        o_ref[...] += part

    @pl.when(f == n_f - 1)
    def _():
        y = alpha * x_ref[...] + FFN_RES_WEIGHT * o_ref[...]
        out = _layer_norm_rows(y, gain_ref[...], bias_ref[...])
        o_ref[...] = out
        ob_ref[...] = out.astype(BF16)


def _ffn_ln(x, wgu, wd, gain, bias, alpha):
    m, d = x.shape
    n_f = D_FF // FFN_TF
    return pl.pallas_call(
        functools.partial(_ffn_ln_kernel, alpha=alpha, n_f=n_f),
        grid=(m // FFN_TM, n_f),
        in_specs=[
            pl.BlockSpec((FFN_TM, d), lambda i, f: (i, 0)),
            pl.BlockSpec((d, 2 * FFN_TF), lambda i, f: (0, f)),
            pl.BlockSpec((FFN_TF, d), lambda i, f: (f, 0)),
            pl.BlockSpec((1, d), lambda i, f: (0, 0)),
            pl.BlockSpec((1, d), lambda i, f: (0, 0)),
        ],
        out_specs=[
            pl.BlockSpec((FFN_TM, d), lambda i, f: (i, 0)),
            pl.BlockSpec((FFN_TM, d), lambda i, f: (i, 0)),
        ],
        out_shape=[jax.ShapeDtypeStruct((m, d), F32), jax.ShapeDtypeStruct((m, d), BF16)],
        scratch_shapes=[pltpu.VMEM((FFN_TM, d), BF16)],
        compiler_params=_cparams(("parallel", "arbitrary")),
        name="ffn_ln",
    )(x, wgu, wd, gain, bias)


def _rope_tab_kernel(pos_ref, invf_ref, cos_ref, sin_ref):
    theta = pos_ref[...].astype(F32) * invf_ref[...]
    cos_ref[...] = jnp.cos(theta)
    sin_ref[...] = jnp.sin(theta)


def _rope_tables(pos_col, inv_freq):
    m = pos_col.shape[0]
    half = inv_freq.shape[1]
    return pl.pallas_call(
        _rope_tab_kernel,
        grid=(m // ROPE_TM,),
        in_specs=[
            pl.BlockSpec((ROPE_TM, 1), lambda i: (i, 0)),
            pl.BlockSpec((1, half), lambda i: (0, 0)),
        ],
        out_specs=[
            pl.BlockSpec((ROPE_TM, half), lambda i: (i, 0)),
            pl.BlockSpec((ROPE_TM, half), lambda i: (i, 0)),
        ],
        out_shape=[jax.ShapeDtypeStruct((m, half), F32)] * 2,
        compiler_params=_cparams(("parallel",)),
        name="rope_tab",
    )(pos_col, inv_freq)


def _proj_kernel(x_ref, w_ref, o_ref):
    o_ref[...] = _dot(x_ref[...], w_ref[...]).astype(o_ref.dtype)


def _proj(xb, w, out_dtype, tn):
    m, k = xb.shape
    n = w.shape[1]
    return pl.pallas_call(
        _proj_kernel,
        grid=(m // PROJ_TM, n // tn),
        in_specs=[
            pl.BlockSpec((PROJ_TM, k), lambda i, j: (i, 0)),
            pl.BlockSpec((k, tn), lambda i, j: (0, j)),
        ],
        out_specs=pl.BlockSpec((PROJ_TM, tn), lambda i, j: (i, j)),
        out_shape=jax.ShapeDtypeStruct((m, n), out_dtype),
        compiler_params=_cparams(("parallel", "arbitrary")),
        name="proj",
    )(xb, w)


def _proj_rope_kernel(x_ref, w_ref, cos_ref, sin_ref, o_ref, *, n_q_blocks, k_scale):
    acc = _dot(x_ref[...], w_ref[...])
    scale = jnp.where(pl.program_id(1) >= n_q_blocks, k_scale, 1.0).astype(F32)
    c = cos_ref[...] * scale
    s = sin_ref[...] * scale
    half = RET_HEAD_DIM // 2
    for j in range(acc.shape[1] // RET_HEAD_DIM):
        lo = j * RET_HEAD_DIM
        e = acc[:, lo:lo + half]
        o = acc[:, lo + half:lo + RET_HEAD_DIM]
        o_ref[:, lo:lo + half] = (e * c - o * s).astype(o_ref.dtype)
        o_ref[:, lo + half:lo + RET_HEAD_DIM] = (o * c + e * s).astype(o_ref.dtype)


def _proj_rope(xb, w_qk, cos, sin):
    m, k = xb.shape
    n = w_qk.shape[1]
    half = RET_HEAD_DIM // 2
    return pl.pallas_call(
        functools.partial(_proj_rope_kernel, n_q_blocks=RET_WIDTH // PROJ_TN,
                          k_scale=RET_HEAD_DIM ** -0.5),
        grid=(m // PROJ_TM, n // PROJ_TN),
        in_specs=[
            pl.BlockSpec((PROJ_TM, k), lambda i, j: (i, 0)),
            pl.BlockSpec((k, PROJ_TN), lambda i, j: (0, j)),
            pl.BlockSpec((PROJ_TM, half), lambda i, j: (i, 0)),
            pl.BlockSpec((PROJ_TM, half), lambda i, j: (i, 0)),
        ],
        out_specs=pl.BlockSpec((PROJ_TM, PROJ_TN), lambda i, j: (i, j)),
        out_shape=jax.ShapeDtypeStruct((m, n), BF16),
        compiler_params=_cparams(("parallel", "arbitrary")),
        name="proj_rope",
    )(xb, w_qk, cos, sin)


def _ret_log_gamma(h):
    return math.log(1.0 - 2.0 ** (-5.0 - h))


def _retention_kernel(q_ref, k_ref, v_ref, g_ref, dmask_ref, qd_ref, kd_ref, gain_ref, bias_ref,
                      o_ref, state_ref):
    @pl.when(pl.program_id(1) == 0)
    def _():
        state_ref[...] = jnp.zeros_like(state_ref)

    tc = q_ref.shape[0]
    dh = RET_HEAD_DIM
    for h in range(RET_HEADS):
        sl = slice(h * dh, (h + 1) * dh)
        q = q_ref[:, sl]
        k = k_ref[:, sl]
        v = v_ref[:, sl]
        scores = lax.dot_general(q, k, (((1,), (1,)), ((), ())), preferred_element_type=F32)
        p = (scores * dmask_ref[h]).astype(BF16)
        state = state_ref[h]
        o = _dot(p, v) + _dot(q, state.astype(BF16)) * qd_ref[:, sl]
        vkd = (v.astype(F32) * kd_ref[:, sl]).astype(BF16)
        kv = lax.dot_general(k, vkd, (((0,), (0,)), ((), ())), preferred_element_type=F32)
        state_ref[h] = state * math.exp(_ret_log_gamma(h) * tc) + kv
        mu = jnp.mean(o, axis=-1, keepdims=True)
        d = o - mu
        var = jnp.mean(d * d, axis=-1, keepdims=True)
        on = d * lax.rsqrt(var + LN_EPS) * gain_ref[:, sl] + bias_ref[:, sl]
        o_ref[:, sl] = (_silu(g_ref[:, sl].astype(F32)) * on).astype(o_ref.dtype)


def _retention(qk, rest, dmask, qd, kd, gain, bias, batch, seq):
    tc = RET_CHUNK
    nc = seq // tc
    w = RET_WIDTH
    row = lambda b, c: b * nc + c
    const2 = lambda b, c: (0, 0)
    return pl.pallas_call(
        _retention_kernel,
        grid=(batch, nc),
        in_specs=[
            pl.BlockSpec((tc, w), lambda b, c: (row(b, c), 0)),
            pl.BlockSpec((tc, w), lambda b, c: (row(b, c), 1)),
            pl.BlockSpec((tc, w), lambda b, c: (row(b, c), 0)),
            pl.BlockSpec((tc, w), lambda b, c: (row(b, c), 1)),
            pl.BlockSpec((RET_HEADS, tc, tc), lambda b, c: (0, 0, 0)),
            pl.BlockSpec((tc, w), const2),
            pl.BlockSpec((tc, w), const2),
            pl.BlockSpec((1, w), const2),
            pl.BlockSpec((1, w), const2),
        ],
        out_specs=pl.BlockSpec((tc, w), lambda b, c: (row(b, c), 0)),
        out_shape=jax.ShapeDtypeStruct((batch * seq, w), BF16),
        scratch_shapes=[pltpu.VMEM((RET_HEADS, RET_HEAD_DIM, RET_HEAD_DIM), F32)],
        compiler_params=_cparams(("parallel", "arbitrary")),
        name="retention",
    )(qk, qk, rest, rest, dmask, qd, kd, gain, bias)


def _retention_consts(tc):
    lg = jnp.asarray([_ret_log_gamma(h) for h in range(RET_HEADS)], F32)
    pos = jnp.arange(tc, dtype=F32)
    rel = pos[:, None] - pos[None, :]
    dmask = jnp.where(rel >= 0, jnp.exp(lg[:, None, None] * jnp.maximum(rel, 0.0)), 0.0)
    qd = jnp.exp(lg[None, :] * (pos[:, None] + 1.0))
    kd = jnp.exp(lg[None, :] * (tc - 1.0 - pos[:, None]))
    rep = lambda t: jnp.repeat(t, RET_HEAD_DIM, axis=1)
    return dmask.astype(F32), rep(qd).astype(F32), rep(kd).astype(F32)


def _ssd_kernel(z_ref, xbc_ref, dtr_ref, convw_ref, convb_ref, dtb_ref, alog_ref, dexp_ref, ngain_ref,
                expand_ref, o_ref, tail_ref, state_ref, xc_ref):
    @pl.when(pl.program_id(1) == 0)
    def _():
        tail_ref[...] = jnp.zeros_like(tail_ref)
        state_ref[...] = jnp.zeros_like(state_ref)

    t = xbc_ref.shape[0]
    ts = SSD_CHUNK
    n = SSD_STATE
    gw = SSD_WIDTH // SSD_GROUPS

    xf = xbc_ref[...].astype(F32)
    prev = tail_ref[...]
    row8 = lax.broadcasted_iota(jnp.int32, prev.shape, 0)
    conv = convb_ref[...] + convw_ref[CONV_WIDTH - 1:CONV_WIDTH, :] * xf
    top = conv[0:SUBLANES, :]
    for s in range(1, CONV_WIDTH):
        w_s = convw_ref[CONV_WIDTH - 1 - s:CONV_WIDTH - s, :]
        rolled = pltpu.roll(xf, s, 0)
        conv = conv + w_s * rolled
        top = top + w_s * jnp.where(row8 < s, pltpu.roll(prev, s, 0), rolled[0:SUBLANES, :])
    xc_ref[...] = _silu(conv)
    xc_ref[0:SUBLANES, :] = _silu(top)
    tail_ref[...] = xf[t - SUBLANES:t, :]

    ri = lax.broadcasted_iota(jnp.int32, (ts, ts), 0)
    ci = lax.broadcasted_iota(jnp.int32, (ts, ts), 1)
    causal = ri >= ci
    tri = jnp.where(causal, 1.0, 0.0).astype(BF16)
    lane_lo = ci < SSD_HEAD_DIM
    expand = expand_ref[...]
    a_neg = -jnp.exp(alog_ref[...])

    for j in range(t // ts):
        rows = slice(j * ts, (j + 1) * ts)
        xs = xc_ref[rows, 0:SSD_WIDTH]
        bm = xc_ref[rows, SSD_WIDTH:SSD_WIDTH + SSD_GROUPS * n].astype(BF16)
        cm = xc_ref[rows, SSD_WIDTH + SSD_GROUPS * n:CONV_CH].astype(BF16)

        pre = dtr_ref[rows, :] + dtb_ref[...]
        dt = jnp.maximum(pre, 0.0) + jnp.log1p(jnp.exp(-jnp.abs(pre)))
        da = dt * a_neg
        acum = _dot_exact01(tri, da, left=True)
        acum_t = acum.T
        acum_e = _dot_exact01(expand, acum, left=False)
        dt_e = _dot_exact01(expand, dt, left=False)
        last_e = acum_e[ts - 1:ts, :]
        from_start = jnp.exp(acum_e)
        to_end = jnp.exp(last_e - acum_e)
        total = jnp.exp(last_e)

        xdt = xs * dt_e
        xdt_te = (xdt * to_end).astype(BF16)

        zf = z_ref[rows, :].astype(F32)
        gate = _silu(zf)
        for g in range(SSD_GROUPS):
            cg = cm[:, g * n:(g + 1) * n]
            bg = bm[:, g * n:(g + 1) * n]
            cb = lax.dot_general(cg, bg, (((1,), (1,)), ((), ())), preferred_element_type=F32)
            intra = []
            for pr in range(gw // LANES):
                lanes = slice(g * gw + pr * LANES, g * gw + (pr + 1) * LANES)
                xpair = xdt[:, lanes]
                y_pair = None
                for half in range(2):
                    hd = (g * gw + pr * LANES) // SSD_HEAD_DIM + half
                    seg = acum[:, hd:hd + 1] - acum_t[hd:hd + 1, :]
                    decay = jnp.exp(jnp.where(causal, seg, -1e30))
                    wmat = (cb * decay).astype(BF16)
                    keep = lane_lo if half == 0 else jnp.logical_not(lane_lo)
                    xh = jnp.where(keep, xpair, 0.0).astype(BF16)
                    term = _dot(wmat, xh)
                    y_pair = term if y_pair is None else y_pair + term
                intra.append(y_pair)
            y_intra = jnp.concatenate(intra, axis=1)
            gl = slice(g * gw, (g + 1) * gw)
            st = state_ref[:, gl]
            y_inter = _dot(cg, st.astype(BF16)) * from_start[:, gl]
            upd = lax.dot_general(bg, xdt_te[:, gl], (((0,), (0,)), ((), ())), preferred_element_type=F32)
            state_ref[:, gl] = st * total[:, gl] + upd
            y = (y_intra + y_inter + xs[:, gl] * dexp_ref[:, gl]) * gate[:, gl]
            ms = jnp.mean(y * y, axis=-1, keepdims=True)
            o_ref[rows, gl] = (y * lax.rsqrt(ms + LN_EPS) * ngain_ref[:, gl]).astype(o_ref.dtype)


def _ssd(rest, dtr, convw, convb, dtb, a_log, dexp, ngain, expand, batch, seq):
    t = SSD_STEP
    nc = seq // t
    row = lambda b, c: b * nc + c
    const2 = lambda b, c: (0, 0)
    return pl.pallas_call(
        _ssd_kernel,
        grid=(batch, nc),
        in_specs=[
            pl.BlockSpec((t, SSD_WIDTH), lambda b, c: (row(b, c), 2)),
            pl.BlockSpec((t, CONV_CH), lambda b, c: (row(b, c), 2)),
            pl.BlockSpec((t, LANES), lambda b, c: (row(b, c), 0)),
            pl.BlockSpec((CONV_WIDTH, CONV_CH), const2),
            pl.BlockSpec((1, CONV_CH), const2),
            pl.BlockSpec((1, LANES), const2),
            pl.BlockSpec((1, LANES), const2),
            pl.BlockSpec((1, SSD_WIDTH), const2),
            pl.BlockSpec((1, SSD_WIDTH), const2),
            pl.BlockSpec((LANES, SSD_WIDTH), const2),
        ],
        out_specs=pl.BlockSpec((t, SSD_WIDTH), lambda b, c: (row(b, c), 0)),
        out_shape=jax.ShapeDtypeStruct((batch * seq, SSD_WIDTH), BF16),
        scratch_shapes=[
            pltpu.VMEM((SUBLANES, CONV_CH), F32),
            pltpu.VMEM((SSD_STATE, SSD_WIDTH), F32),
            pltpu.VMEM((t, CONV_CH), F32),
        ],
        compiler_params=_cparams(("parallel", "arbitrary")),
        name="ssd",
    )(rest, rest, dtr, convw, convb, dtb, a_log, dexp, ngain, expand)


def _outproj_ln_kernel(ret_ref, ssd_ref, x_ref, wo_ref, gain_ref, bias_ref, o_ref, *, alpha):
    mix = _dot(ret_ref[...], wo_ref[0:RET_WIDTH, :]) + _dot(ssd_ref[...], wo_ref[RET_WIDTH:, :])
    y = alpha * x_ref[...] + mix
    o_ref[...] = _layer_norm_rows(y, gain_ref[...], bias_ref[...])


def _outproj_ln(ret, ssd, x, wo, gain, bias, alpha):
    m, d = x.shape
    return pl.pallas_call(
        functools.partial(_outproj_ln_kernel, alpha=alpha),
        grid=(m // OUT_TM,),
        in_specs=[
            pl.BlockSpec((OUT_TM, RET_WIDTH), lambda i: (i, 0)),
            pl.BlockSpec((OUT_TM, SSD_WIDTH), lambda i: (i, 0)),
            pl.BlockSpec((OUT_TM, d), lambda i: (i, 0)),
            pl.BlockSpec((RET_WIDTH + SSD_WIDTH, d), lambda i: (0, 0)),
            pl.BlockSpec((1, d), lambda i: (0, 0)),
            pl.BlockSpec((1, d), lambda i: (0, 0)),
        ],
        out_specs=pl.BlockSpec((OUT_TM, d), lambda i: (i, 0)),
        out_shape=jax.ShapeDtypeStruct((m, d), F32),
        compiler_params=_cparams(("parallel",)),
        name="outproj_ln",
    )(ret, ssd, x, wo, gain, bias)


def _gate_up_interleaved(w_gate, w_up):
    d = w_gate.shape[0]
    n_f = D_FF // FFN_TF
    g = w_gate.reshape(d, n_f, FFN_TF)
    u = w_up.reshape(d, n_f, FFN_TF)
    return jnp.concatenate([g, u], axis=2).reshape(d, 2 * D_FF).astype(BF16)


def _even_odd_head_columns(w):
    d = w.shape[0]
    w4 = w.reshape(d, RET_HEADS, RET_HEAD_DIM // 2, 2)
    return jnp.concatenate([w4[..., 0], w4[..., 1]], axis=2).reshape(d, RET_WIDTH)


def kernel(x, positions, ffn1_w_gate, ffn1_w_up, ffn1_w_down, ln1_gain, ln1_bias, mix_w_in, ret_gn_gain, ret_gn_bias, ssd_conv_w, ssd_conv_b, ssd_dt_bias, ssd_a_log, ssd_d, ssd_norm_gain, mix_w_out, ln2_gain, ln2_bias, ffn2_w_gate, ffn2_w_up, ffn2_w_down, ln3_gain, ln3_bias):
    batch, seq, d = x.shape
    depth = ffn1_w_gate.shape[0]
    alpha = (2.0 * depth) ** 0.25
    m = batch * seq
    h = x.reshape(m, d)

    half = RET_HEAD_DIM // 2
    inv_freq = (1.0 / (ROPE_BASE ** jnp.linspace(0.0, 1.0, half, dtype=F32))).reshape(1, half)
    cos, sin = _rope_tables(positions.reshape(m, 1), inv_freq)
    dmask, qd, kd = _retention_consts(RET_CHUNK)
    expand = (jnp.arange(SSD_WIDTH)[None, :] // SSD_HEAD_DIM == jnp.arange(LANES)[:, None]).astype(BF16)
    row = lambda v: v.reshape(1, -1).astype(F32)
    pad_lanes = lambda v: jnp.pad(v.astype(F32), (0, LANES - v.shape[0])).reshape(1, LANES)

    for layer in range(depth):
        w_in = mix_w_in[layer]
        rw = RET_WIDTH
        w_qk = jnp.concatenate([_even_odd_head_columns(w_in[:, 0:rw]),
                                _even_odd_head_columns(w_in[:, rw:2 * rw])], axis=1).astype(BF16)
        n_rest = 2 * rw + SSD_WIDTH + CONV_CH
        w_rest = w_in[:, 2 * rw:2 * rw + n_rest].astype(BF16)
        w_dt = jnp.pad(w_in[:, 2 * rw + n_rest:], ((0, 0), (0, LANES - SSD_HEADS))).astype(BF16)

        h, hb = _ffn_ln(h, _gate_up_interleaved(ffn1_w_gate[layer], ffn1_w_up[layer]),
                        ffn1_w_down[layer].astype(BF16), row(ln1_gain[layer]), row(ln1_bias[layer]), alpha)

        qk = _proj_rope(hb, w_qk, cos, sin)
        rest = _proj(hb, w_rest, BF16, PROJ_TN)
        dtr = _proj(hb, w_dt, F32, LANES)

        ret = _retention(qk, rest, dmask, qd, kd, row(ret_gn_gain[layer]), row(ret_gn_bias[layer]),
                         batch, seq)
        dexp = jnp.repeat(ssd_d[layer].astype(F32), SSD_HEAD_DIM).reshape(1, SSD_WIDTH)
        ssd = _ssd(rest, dtr, ssd_conv_w[layer].astype(F32), row(ssd_conv_b[layer]),
                   pad_lanes(ssd_dt_bias[layer]), pad_lanes(ssd_a_log[layer]), dexp, row(ssd_norm_gain[layer]), expand,
                   batch, seq)

        h = _outproj_ln(ret, ssd, h, mix_w_out[layer].astype(BF16), row(ln2_gain[layer]),
                        row(ln2_bias[layer]), alpha)

        h, _ = _ffn_ln(h, _gate_up_interleaved(ffn2_w_gate[layer], ffn2_w_up[layer]),
                       ffn2_w_down[layer].astype(BF16), row(ln3_gain[layer]), row(ln3_bias[layer]), alpha)

    return h.reshape(batch, seq, d)
```

```python
import functools
import math

import jax
import jax.numpy as jnp
import numpy as np
from jax import lax
from jax.experimental import pallas as pl
from jax.experimental.pallas import tpu as pltpu

F32 = jnp.float32
BF16 = jnp.bfloat16

D_MODEL = 2048
RET_WIDTH = 1024
RET_HEAD_DIM = 256
RET_HEADS = 4
SSD_WIDTH = 1024
SSD_HEAD_DIM = 64
SSD_HEADS = 16
SSD_GROUPS = 2
SSD_STATE = 128
CONV_WIDTH = 4
CONV_CH = SSD_WIDTH + 2 * SSD_GROUPS * SSD_STATE
D_FF = 5632
ROPE_BASE = 10000.0
LN_EPS = 1e-5
FFN_RES_WEIGHT = 0.5

LANES = 128
SUBLANES = 8
VMEM_BYTES_V7X = 64 * 1024 * 1024
VMEM_LIMIT = VMEM_BYTES_V7X - 8 * 1024 * 1024

FFN_TM = 512
FFN_TF = 512
PROJ_TM = 1024
PROJ_TN = 512
OUT_TM = 512
RET_CHUNK = 256
SSD_CHUNK = 128
SSD_STEP = 256
ROPE_TM = 1024


def _cparams(sem):
    return pltpu.CompilerParams(dimension_semantics=sem, vmem_limit_bytes=VMEM_LIMIT)


def _silu(v):
    return v / (1.0 + jnp.exp(-v))


def _layer_norm_rows(y, gain, bias):
    mu = jnp.mean(y, axis=-1, keepdims=True)
    d = y - mu
    var = jnp.mean(d * d, axis=-1, keepdims=True)
    return d * lax.rsqrt(var + LN_EPS) * gain + bias


def _split3(v):
    h0 = v.astype(BF16)
    r1 = v - h0.astype(F32)
    h1 = r1.astype(BF16)
    h2 = (r1 - h1.astype(F32)).astype(BF16)
    return h0, h1, h2


def _dot(a, b):
    return jnp.dot(a, b, preferred_element_type=F32)


def _dot_exact01(m01, v, left=True):
    parts = _split3(v)
    if left:
        return _dot(m01, parts[0]) + _dot(m01, parts[1]) + _dot(m01, parts[2])
    return _dot(parts[0], m01) + _dot(parts[1], m01) + _dot(parts[2], m01)


def _ffn_ln_kernel(x_ref, wg_ref, wu_ref, wd_ref, gain_ref, bias_ref, o_ref, ob_ref, xb_ref, *, alpha, n_f):
    f = pl.program_id(1)

    @pl.when(f == 0)
    def _():
        xb_ref[...] = x_ref[...].astype(BF16)
        o_ref[...] = jnp.zeros_like(o_ref)

    xb = xb_ref[...]
    h = (_silu(_dot(xb, wg_ref[...])) * _dot(xb, wu_ref[...])).astype(BF16)
    o_ref[...] += _dot(h, wd_ref[...])

    @pl.when(f == n_f - 1)
    def _():
        y = alpha * x_ref[...] + FFN_RES_WEIGHT * o_ref[...]
        out = _layer_norm_rows(y, gain_ref[...], bias_ref[...])
        o_ref[...] = out
        ob_ref[...] = out.astype(BF16)


def _ffn_ln(x, wg, wu, wd, gain, bias, alpha):
    m, d = x.shape
    n_f = D_FF // FFN_TF
    return pl.pallas_call(
        functools.partial(_ffn_ln_kernel, alpha=alpha, n_f=n_f),
        grid=(m // FFN_TM, n_f),
        in_specs=[
            pl.BlockSpec((FFN_TM, d), lambda i, f: (i, 0)),
            pl.BlockSpec((d, FFN_TF), lambda i, f: (0, f)),
            pl.BlockSpec((d, FFN_TF), lambda i, f: (0, f)),
            pl.BlockSpec((FFN_TF, d), lambda i, f: (f, 0)),
            pl.BlockSpec((1, d), lambda i, f: (0, 0)),
            pl.BlockSpec((1, d), lambda i, f: (0, 0)),
        ],
        out_specs=[
            pl.BlockSpec((FFN_TM, d), lambda i, f: (i, 0)),
            pl.BlockSpec((FFN_TM, d), lambda i, f: (i, 0)),
        ],
        out_shape=[jax.ShapeDtypeStruct((m, d), F32), jax.ShapeDtypeStruct((m, d), BF16)],
        scratch_shapes=[pltpu.VMEM((FFN_TM, d), BF16)],
        compiler_params=_cparams(("parallel", "arbitrary")),
        name="ffn_ln",
    )(x, wg, wu, wd, gain, bias)


def _rope_tab_kernel(pos_ref, invf_ref, cos_ref, sin_ref):
    theta = pos_ref[...].astype(F32) * invf_ref[...]
    cos_ref[...] = jnp.cos(theta)
    sin_ref[...] = jnp.sin(theta)


def _rope_tables(pos_col, inv_freq):
    m = pos_col.shape[0]
    half = inv_freq.shape[1]
    return pl.pallas_call(
        _rope_tab_kernel,
        grid=(m // ROPE_TM,),
        in_specs=[
            pl.BlockSpec((ROPE_TM, 1), lambda i: (i, 0)),
            pl.BlockSpec((1, half), lambda i: (0, 0)),
        ],
        out_specs=[
            pl.BlockSpec((ROPE_TM, half), lambda i: (i, 0)),
            pl.BlockSpec((ROPE_TM, half), lambda i: (i, 0)),
        ],
        out_shape=[jax.ShapeDtypeStruct((m, half), F32)] * 2,
        compiler_params=_cparams(("parallel",)),
        name="rope_tab",
    )(pos_col, inv_freq)


def _proj_kernel(x_ref, w_ref, o_ref):
    o_ref[...] = _dot(x_ref[...], w_ref[...]).astype(o_ref.dtype)


def _proj(xb, w, out_dtype, tn, col0=0, n=None):
    m, k = xb.shape
    n = w.shape[1] if n is None else n
    j0 = col0 // tn
    return pl.pallas_call(
        _proj_kernel,
        grid=(m // PROJ_TM, n // tn),
        in_specs=[
            pl.BlockSpec((PROJ_TM, k), lambda i, j: (i, 0)),
            pl.BlockSpec((k, tn), lambda i, j: (0, j + j0)),
        ],
        out_specs=pl.BlockSpec((PROJ_TM, tn), lambda i, j: (i, j)),
        out_shape=jax.ShapeDtypeStruct((m, n), out_dtype),
        compiler_params=_cparams(("parallel", "arbitrary")),
        name="proj",
    )(xb, w)


def _proj_rope_kernel(x_ref, w_ref, cos_ref, sin_ref, o_ref, *, n_q_blocks, k_scale):
    acc = _dot(x_ref[...], w_ref[...])
    scale = jnp.where(pl.program_id(1) >= n_q_blocks, k_scale, 1.0).astype(F32)
    c = cos_ref[...] * scale
    s = sin_ref[...] * scale
    half = RET_HEAD_DIM // 2
    for j in range(acc.shape[1] // RET_HEAD_DIM):
        lo = j * RET_HEAD_DIM
        e = acc[:, lo:lo + half]
        o = acc[:, lo + half:lo + RET_HEAD_DIM]
        o_ref[:, lo:lo + half] = (e * c - o * s).astype(o_ref.dtype)
        o_ref[:, lo + half:lo + RET_HEAD_DIM] = (o * c + e * s).astype(o_ref.dtype)


def _proj_rope(xb, w_qk, cos, sin):
    m, k = xb.shape
    n = w_qk.shape[1]
    half = RET_HEAD_DIM // 2
    return pl.pallas_call(
        functools.partial(_proj_rope_kernel, n_q_blocks=RET_WIDTH // PROJ_TN,
                          k_scale=RET_HEAD_DIM ** -0.5),
        grid=(m // PROJ_TM, n // PROJ_TN),
        in_specs=[
            pl.BlockSpec((PROJ_TM, k), lambda i, j: (i, 0)),
            pl.BlockSpec((k, PROJ_TN), lambda i, j: (0, j)),
            pl.BlockSpec((PROJ_TM, half), lambda i, j: (i, 0)),
            pl.BlockSpec((PROJ_TM, half), lambda i, j: (i, 0)),
        ],
        out_specs=pl.BlockSpec((PROJ_TM, PROJ_TN), lambda i, j: (i, j)),
        out_shape=jax.ShapeDtypeStruct((m, n), BF16),
        compiler_params=_cparams(("parallel", "arbitrary")),
        name="proj_rope",
    )(xb, w_qk, cos, sin)


def _ret_log_gamma(h):
    return math.log(1.0 - 2.0 ** (-5.0 - h))


def _retention_kernel(q_ref, k_ref, v_ref, g_ref, dmask_ref, qd_ref, kd_ref, gain_ref, bias_ref,
                      o_ref, state_ref):
    @pl.when(pl.program_id(1) == 0)
    def _():
        state_ref[...] = jnp.zeros_like(state_ref)

    tc = q_ref.shape[0]
    dh = RET_HEAD_DIM
    for h in range(RET_HEADS):
        sl = slice(h * dh, (h + 1) * dh)
        q = q_ref[:, sl]
        k = k_ref[:, sl]
        v = v_ref[:, sl]
        scores = lax.dot_general(q, k, (((1,), (1,)), ((), ())), preferred_element_type=F32)
        p = (scores * dmask_ref[h]).astype(BF16)
        state = state_ref[h]
        o = _dot(p, v) + _dot(q, state.astype(BF16)) * qd_ref[:, sl]
        vkd = (v.astype(F32) * kd_ref[:, sl]).astype(BF16)
        kv = lax.dot_general(k, vkd, (((0,), (0,)), ((), ())), preferred_element_type=F32)
        state_ref[h] = state * math.exp(_ret_log_gamma(h) * tc) + kv
        mu = jnp.mean(o, axis=-1, keepdims=True)
        d = o - mu
        var = jnp.mean(d * d, axis=-1, keepdims=True)
        on = d * lax.rsqrt(var + LN_EPS) * gain_ref[:, sl] + bias_ref[:, sl]
        o_ref[:, sl] = (_silu(g_ref[:, sl].astype(F32)) * on).astype(o_ref.dtype)


def _retention(qk, rest, dmask, qd, kd, gain, bias, batch, seq):
    tc = RET_CHUNK
    nc = seq // tc
    w = RET_WIDTH
    row = lambda b, c: b * nc + c
    const2 = lambda b, c: (0, 0)
    return pl.pallas_call(
        _retention_kernel,
        grid=(batch, nc),
        in_specs=[
            pl.BlockSpec((tc, w), lambda b, c: (row(b, c), 0)),
            pl.BlockSpec((tc, w), lambda b, c: (row(b, c), 1)),
            pl.BlockSpec((tc, w), lambda b, c: (row(b, c), 0)),
            pl.BlockSpec((tc, w), lambda b, c: (row(b, c), 1)),
            pl.BlockSpec((RET_HEADS, tc, tc), lambda b, c: (0, 0, 0)),
            pl.BlockSpec((tc, w), const2),
            pl.BlockSpec((tc, w), const2),
            pl.BlockSpec((1, w), const2),
            pl.BlockSpec((1, w), const2),
        ],
        out_specs=pl.BlockSpec((tc, w), lambda b, c: (row(b, c), 0)),
        out_shape=jax.ShapeDtypeStruct((batch * seq, w), BF16),
        scratch_shapes=[pltpu.VMEM((RET_HEADS, RET_HEAD_DIM, RET_HEAD_DIM), F32)],
        compiler_params=_cparams(("parallel", "arbitrary")),
        name="retention",
    )(qk, qk, rest, rest, dmask, qd, kd, gain, bias)


def _retention_consts(tc):
    lg = jnp.asarray([_ret_log_gamma(h) for h in range(RET_HEADS)], F32)
    pos = jnp.arange(tc, dtype=F32)
    rel = pos[:, None] - pos[None, :]
    dmask = jnp.where(rel >= 0, jnp.exp(lg[:, None, None] * jnp.maximum(rel, 0.0)), 0.0)
    qd = jnp.exp(lg[None, :] * (pos[:, None] + 1.0))
    kd = jnp.exp(lg[None, :] * (tc - 1.0 - pos[:, None]))
    rep = lambda t: jnp.repeat(t, RET_HEAD_DIM, axis=1)
    return dmask.astype(F32), rep(qd).astype(F32), rep(kd).astype(F32)


def _ssd_kernel(z_ref, xbc_ref, dtr_ref, convw_ref, convb_ref, dtb_ref, alog_ref, dexp_ref, ngain_ref,
                expand_ref, o_ref, tail_ref, state_ref, xc_ref):
    @pl.when(pl.program_id(1) == 0)
    def _():
        tail_ref[...] = jnp.zeros_like(tail_ref)
        state_ref[...] = jnp.zeros_like(state_ref)

    t = xbc_ref.shape[0]
    ts = SSD_CHUNK
    n = SSD_STATE
    gw = SSD_WIDTH // SSD_GROUPS

    xf = xbc_ref[...].astype(F32)
    prev = tail_ref[...]
    row8 = lax.broadcasted_iota(jnp.int32, prev.shape, 0)
    conv = convb_ref[...] + convw_ref[CONV_WIDTH - 1:CONV_WIDTH, :] * xf
    top = conv[0:SUBLANES, :]
    for s in range(1, CONV_WIDTH):
        w_s = convw_ref[CONV_WIDTH - 1 - s:CONV_WIDTH - s, :]
        rolled = pltpu.roll(xf, s, 0)
        conv = conv + w_s * rolled
        top = top + w_s * jnp.where(row8 < s, pltpu.roll(prev, s, 0), rolled[0:SUBLANES, :])
    xc_ref[...] = _silu(conv)
    xc_ref[0:SUBLANES, :] = _silu(top)
    tail_ref[...] = xf[t - SUBLANES:t, :]

    ri = lax.broadcasted_iota(jnp.int32, (ts, ts), 0)
    ci = lax.broadcasted_iota(jnp.int32, (ts, ts), 1)
    causal = ri >= ci
    tri = jnp.where(causal, 1.0, 0.0).astype(BF16)
    lane_lo = ci < SSD_HEAD_DIM
    expand = expand_ref[...]
    a_neg = -jnp.exp(alog_ref[...])

    for j in range(t // ts):
        rows = slice(j * ts, (j + 1) * ts)
        xs = xc_ref[rows, 0:SSD_WIDTH]
        bm = xc_ref[rows, SSD_WIDTH:SSD_WIDTH + SSD_GROUPS * n].astype(BF16)
        cm = xc_ref[rows, SSD_WIDTH + SSD_GROUPS * n:CONV_CH].astype(BF16)

        pre = dtr_ref[rows, :] + dtb_ref[...]
        dt = jnp.maximum(pre, 0.0) + jnp.log1p(jnp.exp(-jnp.abs(pre)))
        da = dt * a_neg
        acum = _dot_exact01(tri, da, left=True)
        acum_t = acum.T
        acum_e = _dot_exact01(expand, acum, left=False)
        dt_e = _dot_exact01(expand, dt, left=False)
        last_e = acum_e[ts - 1:ts, :]
        from_start = jnp.exp(acum_e)
        to_end = jnp.exp(last_e - acum_e)
        total = jnp.exp(last_e)

        xdt = xs * dt_e
        xdt_te = (xdt * to_end).astype(BF16)

        zf = z_ref[rows, :].astype(F32)
        gate = _silu(zf)
        for g in range(SSD_GROUPS):
            cg = cm[:, g * n:(g + 1) * n]
            bg = bm[:, g * n:(g + 1) * n]
            cb = lax.dot_general(cg, bg, (((1,), (1,)), ((), ())), preferred_element_type=F32)
            intra = []
            for pr in range(gw // LANES):
                lanes = slice(g * gw + pr * LANES, g * gw + (pr + 1) * LANES)
                xpair = xdt[:, lanes]
                y_pair = None
                for half in range(2):
                    hd = (g * gw + pr * LANES) // SSD_HEAD_DIM + half
                    seg = acum[:, hd:hd + 1] - acum_t[hd:hd + 1, :]
                    decay = jnp.exp(jnp.where(causal, seg, -1e30))
                    wmat = (cb * decay).astype(BF16)
                    keep = lane_lo if half == 0 else jnp.logical_not(lane_lo)
                    xh = jnp.where(keep, xpair, 0.0).astype(BF16)
                    term = _dot(wmat, xh)
                    y_pair = term if y_pair is None else y_pair + term
                intra.append(y_pair)
            y_intra = jnp.concatenate(intra, axis=1)
            gl = slice(g * gw, (g + 1) * gw)
            st = state_ref[:, gl]
            y_inter = _dot(cg, st.astype(BF16)) * from_start[:, gl]
            upd = lax.dot_general(bg, xdt_te[:, gl], (((0,), (0,)), ((), ())), preferred_element_type=F32)
            state_ref[:, gl] = st * total[:, gl] + upd
            y = (y_intra + y_inter + xs[:, gl] * dexp_ref[:, gl]) * gate[:, gl]
            ms = jnp.mean(y * y, axis=-1, keepdims=True)
            o_ref[rows, gl] = (y * lax.rsqrt(ms + LN_EPS) * ngain_ref[:, gl]).astype(o_ref.dtype)


def _ssd(rest, dtr, convw, convb, dtb, a_log, dexp, ngain, expand, batch, seq):
    t = SSD_STEP
    nc = seq // t
    row = lambda b, c: b * nc + c
    const2 = lambda b, c: (0, 0)
    return pl.pallas_call(
        _ssd_kernel,
        grid=(batch, nc),
        in_specs=[
            pl.BlockSpec((t, SSD_WIDTH), lambda b, c: (row(b, c), 2)),
            pl.BlockSpec((t, CONV_CH), lambda b, c: (row(b, c), 2)),
            pl.BlockSpec((t, LANES), lambda b, c: (row(b, c), 0)),
            pl.BlockSpec((CONV_WIDTH, CONV_CH), const2),
            pl.BlockSpec((1, CONV_CH), const2),
            pl.BlockSpec((1, LANES), const2),
            pl.BlockSpec((1, LANES), const2),
            pl.BlockSpec((1, SSD_WIDTH), const2),
            pl.BlockSpec((1, SSD_WIDTH), const2),
            pl.BlockSpec((LANES, SSD_WIDTH), const2),
        ],
        out_specs=pl.BlockSpec((t, SSD_WIDTH), lambda b, c: (row(b, c), 0)),
        out_shape=jax.ShapeDtypeStruct((batch * seq, SSD_WIDTH), BF16),
        scratch_shapes=[
            pltpu.VMEM((SUBLANES, CONV_CH), F32),
            pltpu.VMEM((SSD_STATE, SSD_WIDTH), F32),
            pltpu.VMEM((t, CONV_CH), F32),
        ],
        compiler_params=_cparams(("parallel", "arbitrary")),
        name="ssd",
    )(rest, rest, dtr, convw, convb, dtb, a_log, dexp, ngain, expand)


def _outproj_ln_kernel(ret_ref, ssd_ref, x_ref, wo_ref, gain_ref, bias_ref, o_ref, *, alpha):
    mix = _dot(ret_ref[...], wo_ref[0:RET_WIDTH, :]) + _dot(ssd_ref[...], wo_ref[RET_WIDTH:, :])
    y = alpha * x_ref[...] + mix
    o_ref[...] = _layer_norm_rows(y, gain_ref[...], bias_ref[...])


def _outproj_ln(ret, ssd, x, wo, gain, bias, alpha):
    m, d = x.shape
    return pl.pallas_call(
        functools.partial(_outproj_ln_kernel, alpha=alpha),
        grid=(m // OUT_TM,),
        in_specs=[
            pl.BlockSpec((OUT_TM, RET_WIDTH), lambda i: (i, 0)),
            pl.BlockSpec((OUT_TM, SSD_WIDTH), lambda i: (i, 0)),
            pl.BlockSpec((OUT_TM, d), lambda i: (i, 0)),
            pl.BlockSpec((RET_WIDTH + SSD_WIDTH, d), lambda i: (0, 0)),
            pl.BlockSpec((1, d), lambda i: (0, 0)),
            pl.BlockSpec((1, d), lambda i: (0, 0)),
        ],
        out_specs=pl.BlockSpec((OUT_TM, d), lambda i: (i, 0)),
        out_shape=jax.ShapeDtypeStruct((m, d), F32),
        compiler_params=_cparams(("parallel",)),
        name="outproj_ln",
    )(ret, ssd, x, wo, gain, bias)


def _even_odd_head_columns(w):
    d = w.shape[0]
    w4 = w.reshape(d, RET_HEADS, RET_HEAD_DIM // 2, 2)
    return jnp.concatenate([w4[..., 0], w4[..., 1]], axis=2).reshape(d, RET_WIDTH)


def kernel(x, positions, ffn1_w_gate, ffn1_w_up, ffn1_w_down, ln1_gain, ln1_bias, mix_w_in, ret_gn_gain, ret_gn_bias, ssd_conv_w, ssd_conv_b, ssd_dt_bias, ssd_a_log, ssd_d, ssd_norm_gain, mix_w_out, ln2_gain, ln2_bias, ffn2_w_gate, ffn2_w_up, ffn2_w_down, ln3_gain, ln3_bias):
    batch, seq, d = x.shape
    depth = ffn1_w_gate.shape[0]
    alpha = (2.0 * depth) ** 0.25
    m = batch * seq
    h = x.reshape(m, d)

    half = RET_HEAD_DIM // 2
    inv_freq = (1.0 / (ROPE_BASE ** jnp.linspace(0.0, 1.0, half, dtype=F32))).reshape(1, half)
    cos, sin = _rope_tables(positions.reshape(m, 1), inv_freq)
    dmask, qd, kd = _retention_consts(RET_CHUNK)
    expand = (jnp.arange(SSD_WIDTH)[None, :] // SSD_HEAD_DIM == jnp.arange(LANES)[:, None]).astype(BF16)
    row = lambda v: v.reshape(1, -1).astype(F32)
    pad_lanes = lambda v: jnp.pad(v.astype(F32), (0, LANES - v.shape[0])).reshape(1, LANES)

    for layer in range(depth):
        w_in = mix_w_in[layer].astype(BF16)
        rw = RET_WIDTH
        w_qk = jnp.concatenate([_even_odd_head_columns(w_in[:, 0:rw]),
                                _even_odd_head_columns(w_in[:, rw:2 * rw])], axis=1)
        n_rest = 2 * rw + SSD_WIDTH + CONV_CH
        w_dt = jnp.pad(w_in[:, 2 * rw + n_rest:], ((0, 0), (0, LANES - SSD_HEADS)))

        h, hb = _ffn_ln(h, ffn1_w_gate[layer].astype(BF16), ffn1_w_up[layer].astype(BF16),
                        ffn1_w_down[layer].astype(BF16), row(ln1_gain[layer]), row(ln1_bias[layer]), alpha)

        qk = _proj_rope(hb, w_qk, cos, sin)
        rest = _proj(hb, w_in, BF16, PROJ_TN, col0=2 * rw, n=n_rest)
        dtr = _proj(hb, w_dt, F32, LANES)

        ret = _retention(qk, rest, dmask, qd, kd, row(ret_gn_gain[layer]), row(ret_gn_bias[layer]),
                         batch, seq)
        dexp = jnp.repeat(ssd_d[layer].astype(F32), SSD_HEAD_DIM).reshape(1, SSD_WIDTH)
        ssd = _ssd(rest, dtr, ssd_conv_w[layer].astype(F32), row(ssd_conv_b[layer]),
                   pad_lanes(ssd_dt_bias[layer]), pad_lanes(ssd_a_log[layer]), dexp, row(ssd_norm_gain[layer]), expand,
                   batch, seq)

        h = _outproj_ln(ret, ssd, h, mix_w_out[layer].astype(BF16), row(ln2_gain[layer]),
                        row(ln2_bias[layer]), alpha)

        h, _ = _ffn_ln(h, ffn2_w_gate[layer].astype(BF16), ffn2_w_up[layer].astype(BF16),
                       ffn2_w_down[layer].astype(BF16), row(ln3_gain[layer]), row(ln3_bias[layer]), alpha)

    return h.reshape(batch, seq, d)
```

```python
import functools
import math

import jax
import jax.numpy as jnp
import numpy as np
from jax import lax
from jax.experimental import pallas as pl
from jax.experimental.pallas import tpu as pltpu

F32 = jnp.float32
BF16 = jnp.bfloat16

D_MODEL = 2048
RET_WIDTH = 1024
RET_HEAD_DIM = 256
RET_HEADS = 4
SSD_WIDTH = 1024
SSD_HEAD_DIM = 64
SSD_HEADS = 16
SSD_GROUPS = 2
SSD_STATE = 128
CONV_WIDTH = 4
CONV_CH = SSD_WIDTH + 2 * SSD_GROUPS * SSD_STATE
D_FF = 5632
ROPE_BASE = 10000.0
LN_EPS = 1e-5
FFN_RES_WEIGHT = 0.5

LANES = 128
SUBLANES = 8
VMEM_BYTES_V7X = 64 * 1024 * 1024
VMEM_LIMIT = VMEM_BYTES_V7X - 4 * 1024 * 1024

FFN_TM = 1024
FFN_TF = 512
PROJ_TM = 512
PROJ_TN_QK = 1024
PROJ_TN_REST = 1536
OUT_TM = 512
RET_CHUNK = 256
SSD_CHUNK = 128
SSD_STEP = 256
ROPE_TM = 1024


def _cparams(sem):
    return pltpu.CompilerParams(dimension_semantics=sem, vmem_limit_bytes=VMEM_LIMIT)


def _silu(v):
    return v / (1.0 + jnp.exp(-v))


def _layer_norm_rows(y, gain, bias):
    mu = jnp.mean(y, axis=-1, keepdims=True)
    d = y - mu
    var = jnp.mean(d * d, axis=-1, keepdims=True)
    return d * lax.rsqrt(var + LN_EPS) * gain + bias


def _split3(v):
    h0 = v.astype(BF16)
    r1 = v - h0.astype(F32)
    h1 = r1.astype(BF16)
    h2 = (r1 - h1.astype(F32)).astype(BF16)
    return h0, h1, h2


def _dot(a, b):
    return jnp.dot(a, b, preferred_element_type=F32)


def _dot_exact01(m01, v, left=True):
    parts = _split3(v)
    if left:
        return _dot(m01, parts[0]) + _dot(m01, parts[1]) + _dot(m01, parts[2])
    return _dot(parts[0], m01) + _dot(parts[1], m01) + _dot(parts[2], m01)


def _ffn_ln_kernel(x_ref, wg_ref, wu_ref, wd_ref, gain_ref, bias_ref, o_ref, xb_ref, *, alpha, n_f):
    f = pl.program_id(1)

    @pl.when(f == 0)
    def _():
        xb_ref[...] = x_ref[...].astype(BF16)
        o_ref[...] = jnp.zeros_like(o_ref)

    xb = xb_ref[...]
    h = (_silu(_dot(xb, wg_ref[...])) * _dot(xb, wu_ref[...])).astype(BF16)
    o_ref[...] += _dot(h, wd_ref[...])

    @pl.when(f == n_f - 1)
    def _():
        y = alpha * x_ref[...] + FFN_RES_WEIGHT * o_ref[...]
        o_ref[...] = _layer_norm_rows(y, gain_ref[...], bias_ref[...])


def _ffn_ln(x, wg, wu, wd, gain, bias, alpha):
    m, d = x.shape
    n_f = D_FF // FFN_TF
    return pl.pallas_call(
        functools.partial(_ffn_ln_kernel, alpha=alpha, n_f=n_f),
        grid=(m // FFN_TM, n_f),
        in_specs=[
            pl.BlockSpec((FFN_TM, d), lambda i, f: (i, 0), pipeline_mode=pl.Buffered(1)),
            pl.BlockSpec((d, FFN_TF), lambda i, f: (0, f)),
            pl.BlockSpec((d, FFN_TF), lambda i, f: (0, f)),
            pl.BlockSpec((FFN_TF, d), lambda i, f: (f, 0)),
            pl.BlockSpec((1, d), lambda i, f: (0, 0)),
            pl.BlockSpec((1, d), lambda i, f: (0, 0)),
        ],
        out_specs=pl.BlockSpec((FFN_TM, d), lambda i, f: (i, 0)),
        out_shape=jax.ShapeDtypeStruct((m, d), F32),
        scratch_shapes=[pltpu.VMEM((FFN_TM, d), BF16)],
        compiler_params=_cparams(("parallel", "arbitrary")),
        name="ffn_ln",
    )(x, wg, wu, wd, gain, bias)


def _rope_tab_kernel(pos_ref, invf_ref, cos_ref, sin_ref):
    theta = pos_ref[...].astype(F32) * invf_ref[...]
    cos_ref[...] = jnp.cos(theta)
    sin_ref[...] = jnp.sin(theta)


def _rope_tables(pos_col, inv_freq):
    m = pos_col.shape[0]
    half = inv_freq.shape[1]
    return pl.pallas_call(
        _rope_tab_kernel,
        grid=(m // ROPE_TM,),
        in_specs=[
            pl.BlockSpec((ROPE_TM, 1), lambda i: (i, 0)),
            pl.BlockSpec((1, half), lambda i: (0, 0)),
        ],
        out_specs=[
            pl.BlockSpec((ROPE_TM, half), lambda i: (i, 0)),
            pl.BlockSpec((ROPE_TM, half), lambda i: (i, 0)),
        ],
        out_shape=[jax.ShapeDtypeStruct((m, half), F32)] * 2,
        compiler_params=_cparams(("parallel",)),
        name="rope_tab",
    )(pos_col, inv_freq)


def _in_proj_kernel(x_ref, wqk_ref, wrest_ref, wdt_ref, cos_ref, sin_ref, qk_ref, rest_ref, dt_ref, xb_ref,
                    *, n_qk, k_scale):
    j = pl.program_id(1)

    @pl.when(j == 0)
    def _():
        xb_ref[...] = x_ref[...].astype(BF16)
        dt_ref[...] = _dot(xb_ref[...], wdt_ref[...])

    @pl.when(j < n_qk)
    def _():
        acc = _dot(xb_ref[...], wqk_ref[...])
        scale = jnp.where(j >= n_qk // 2, k_scale, 1.0).astype(F32)
        c = cos_ref[...] * scale
        s = sin_ref[...] * scale
        half = RET_HEAD_DIM // 2
        for hd in range(acc.shape[1] // RET_HEAD_DIM):
            lo = hd * RET_HEAD_DIM
            e = acc[:, lo:lo + half]
            o = acc[:, lo + half:lo + RET_HEAD_DIM]
            qk_ref[:, lo:lo + half] = (e * c - o * s).astype(qk_ref.dtype)
            qk_ref[:, lo + half:lo + RET_HEAD_DIM] = (o * c + e * s).astype(qk_ref.dtype)

    @pl.when(j >= n_qk)
    def _():
        rest_ref[...] = _dot(xb_ref[...], wrest_ref[...]).astype(rest_ref.dtype)


def _in_proj(x, w_qk, w_rest, w_dt, cos, sin):
    m, k = x.shape
    half = RET_HEAD_DIM // 2
    n_qk = w_qk.shape[1] // PROJ_TN_QK
    n_rest = w_rest.shape[1] // PROJ_TN_REST
    qk_idx = lambda i, j: (0, jnp.minimum(j, n_qk - 1))
    rest_idx = lambda i, j: (0, jnp.maximum(j - n_qk, 0))
    return pl.pallas_call(
        functools.partial(_in_proj_kernel, n_qk=n_qk, k_scale=RET_HEAD_DIM ** -0.5),
        grid=(m // PROJ_TM, n_qk + n_rest),
        in_specs=[
            pl.BlockSpec((PROJ_TM, k), lambda i, j: (i, 0)),
            pl.BlockSpec((k, PROJ_TN_QK), qk_idx),
            pl.BlockSpec((k, PROJ_TN_REST), rest_idx),
            pl.BlockSpec((k, LANES), lambda i, j: (0, 0)),
            pl.BlockSpec((PROJ_TM, half), lambda i, j: (i, 0)),
            pl.BlockSpec((PROJ_TM, half), lambda i, j: (i, 0)),
        ],
        out_specs=[
            pl.BlockSpec((PROJ_TM, PROJ_TN_QK), lambda i, j: (i, qk_idx(i, j)[1])),
            pl.BlockSpec((PROJ_TM, PROJ_TN_REST), lambda i, j: (i, rest_idx(i, j)[1])),
            pl.BlockSpec((PROJ_TM, LANES), lambda i, j: (i, 0)),
        ],
        out_shape=[
            jax.ShapeDtypeStruct((m, w_qk.shape[1]), BF16),
            jax.ShapeDtypeStruct((m, w_rest.shape[1]), BF16),
            jax.ShapeDtypeStruct((m, LANES), F32),
        ],
        scratch_shapes=[pltpu.VMEM((PROJ_TM, k), BF16)],
        compiler_params=_cparams(("parallel", "arbitrary")),
        name="in_proj",
    )(x, w_qk, w_rest, w_dt, cos, sin)


def _ret_log_gamma(h):
    return math.log(1.0 - 2.0 ** (-5.0 - h))


def _retention_kernel(q_ref, k_ref, v_ref, g_ref, dmask_ref, qd_ref, kd_ref, gain_ref, bias_ref,
                      o_ref, state_ref):
    @pl.when(pl.program_id(1) == 0)
    def _():
        state_ref[...] = jnp.zeros_like(state_ref)

    tc = q_ref.shape[0]
    dh = RET_HEAD_DIM
    for h in range(RET_HEADS):
        sl = slice(h * dh, (h + 1) * dh)
        q = q_ref[:, sl]
        k = k_ref[:, sl]
        v = v_ref[:, sl]
        scores = lax.dot_general(q, k, (((1,), (1,)), ((), ())), preferred_element_type=F32)
        p = (scores * dmask_ref[h]).astype(BF16)
        state = state_ref[h]
        o = _dot(p, v) + _dot(q, state.astype(BF16)) * qd_ref[:, sl]
        vkd = (v.astype(F32) * kd_ref[:, sl]).astype(BF16)
        kv = lax.dot_general(k, vkd, (((0,), (0,)), ((), ())), preferred_element_type=F32)
        state_ref[h] = state * math.exp(_ret_log_gamma(h) * tc) + kv
        mu = jnp.mean(o, axis=-1, keepdims=True)
        d = o - mu
        var = jnp.mean(d * d, axis=-1, keepdims=True)
        on = d * lax.rsqrt(var + LN_EPS) * gain_ref[:, sl] + bias_ref[:, sl]
        o_ref[:, sl] = (_silu(g_ref[:, sl].astype(F32)) * on).astype(o_ref.dtype)


def _retention(qk, rest, dmask, qd, kd, gain, bias, batch, seq):
    tc = RET_CHUNK
    nc = seq // tc
    w = RET_WIDTH
    row = lambda b, c: b * nc + c
    const2 = lambda b, c: (0, 0)
    return pl.pallas_call(
        _retention_kernel,
        grid=(batch, nc),
        in_specs=[
            pl.BlockSpec((tc, w), lambda b, c: (row(b, c), 0)),
            pl.BlockSpec((tc, w), lambda b, c: (row(b, c), 1)),
            pl.BlockSpec((tc, w), lambda b, c: (row(b, c), 0)),
            pl.BlockSpec((tc, w), lambda b, c: (row(b, c), 1)),
            pl.BlockSpec((RET_HEADS, tc, tc), lambda b, c: (0, 0, 0)),
            pl.BlockSpec((tc, w), const2),
            pl.BlockSpec((tc, w), const2),
            pl.BlockSpec((1, w), const2),
            pl.BlockSpec((1, w), const2),
        ],
        out_specs=pl.BlockSpec((tc, w), lambda b, c: (row(b, c), 0)),
        out_shape=jax.ShapeDtypeStruct((batch * seq, w), BF16),
        scratch_shapes=[pltpu.VMEM((RET_HEADS, RET_HEAD_DIM, RET_HEAD_DIM), F32)],
        compiler_params=_cparams(("parallel", "arbitrary")),
        name="retention",
    )(qk, qk, rest, rest, dmask, qd, kd, gain, bias)


def _retention_consts(tc):
    lg = jnp.asarray([_ret_log_gamma(h) for h in range(RET_HEADS)], F32)
    pos = jnp.arange(tc, dtype=F32)
    rel = pos[:, None] - pos[None, :]
    dmask = jnp.where(rel >= 0, jnp.exp(lg[:, None, None] * jnp.maximum(rel, 0.0)), 0.0)
    qd = jnp.exp(lg[None, :] * (pos[:, None] + 1.0))
    kd = jnp.exp(lg[None, :] * (tc - 1.0 - pos[:, None]))
    rep = lambda t: jnp.repeat(t, RET_HEAD_DIM, axis=1)
    return dmask.astype(F32), rep(qd).astype(F32), rep(kd).astype(F32)


def _ssd_kernel(z_ref, xbc_ref, dtr_ref, convw_ref, convb_ref, dtb_ref, alog_ref, dexp_ref, ngain_ref,
                expand_ref, o_ref, xpad_ref, state_ref, xc_ref):
    @pl.when(pl.program_id(1) == 0)
    def _():
        xpad_ref[0:SUBLANES, :] = jnp.zeros((SUBLANES, CONV_CH), F32)
        state_ref[...] = jnp.zeros_like(state_ref)

    t = xbc_ref.shape[0]
    ts = SSD_CHUNK
    n = SSD_STATE
    gw = SSD_WIDTH // SSD_GROUPS

    xf = xbc_ref[...].astype(F32)
    xpad_ref[SUBLANES:, :] = xf
    conv = convb_ref[...] + convw_ref[CONV_WIDTH - 1:CONV_WIDTH, :] * xf
    for s in range(1, CONV_WIDTH):
        w_s = convw_ref[CONV_WIDTH - 1 - s:CONV_WIDTH - s, :]
        conv = conv + w_s * xpad_ref[SUBLANES - s:SUBLANES - s + t, :]
    xc_ref[...] = _silu(conv)
    xpad_ref[0:SUBLANES, :] = xf[t - SUBLANES:t, :]

    ri = lax.broadcasted_iota(jnp.int32, (ts, ts), 0)
    ci = lax.broadcasted_iota(jnp.int32, (ts, ts), 1)
    causal = ri >= ci
    tri = jnp.where(causal, 1.0, 0.0).astype(BF16)
    lane_lo = ci < SSD_HEAD_DIM
    expand = expand_ref[...]
    a_neg = -jnp.exp(alog_ref[...])

    for j in range(t // ts):
        rows = slice(j * ts, (j + 1) * ts)
        xs = xc_ref[rows, 0:SSD_WIDTH]
        bm = xc_ref[rows, SSD_WIDTH:SSD_WIDTH + SSD_GROUPS * n].astype(BF16)
        cm = xc_ref[rows, SSD_WIDTH + SSD_GROUPS * n:CONV_CH].astype(BF16)

        pre = dtr_ref[rows, :] + dtb_ref[...]
        dt = jnp.maximum(pre, 0.0) + jnp.log1p(jnp.exp(-jnp.abs(pre)))
        da = dt * a_neg
        acum = _dot_exact01(tri, da, left=True)
        acum_t = acum.T
        acum_e = _dot_exact01(expand, acum, left=False)
        dt_e = _dot_exact01(expand, dt, left=False)
        last_e = acum_e[ts - 1:ts, :]
        from_start = jnp.exp(acum_e)
        to_end = jnp.exp(last_e - acum_e)
        total = jnp.exp(last_e)

        xdt = xs * dt_e
        xdt_te = (xdt * to_end).astype(BF16)

        zf = z_ref[rows, :].astype(F32)
        gate = _silu(zf)
        for g in range(SSD_GROUPS):
            cg = cm[:, g * n:(g + 1) * n]
            bg = bm[:, g * n:(g + 1) * n]
            cb = lax.dot_general(cg, bg, (((1,), (1,)), ((), ())), preferred_element_type=F32)
            intra = []
            for pr in range(gw // LANES):
                lanes = slice(g * gw + pr * LANES, g * gw + (pr + 1) * LANES)
                xpair = xdt[:, lanes]
                y_pair = None
                for half in range(2):
                    hd = (g * gw + pr * LANES) // SSD_HEAD_DIM + half
                    seg = acum[:, hd:hd + 1] - acum_t[hd:hd + 1, :]
                    decay = jnp.exp(jnp.where(causal, seg, -1e30))
                    wmat = (cb * decay).astype(BF16)
                    keep = lane_lo if half == 0 else jnp.logical_not(lane_lo)
                    xh = jnp.where(keep, xpair, 0.0).astype(BF16)
                    term = _dot(wmat, xh)
                    y_pair = term if y_pair is None else y_pair + term
                intra.append(y_pair)
            y_intra = jnp.concatenate(intra, axis=1)
            gl = slice(g * gw, (g + 1) * gw)
            st = state_ref[:, gl]
            y_inter = _dot(cg, st.astype(BF16)) * from_start[:, gl]
            upd = lax.dot_general(bg, xdt_te[:, gl], (((0,), (0,)), ((), ())), preferred_element_type=F32)
            state_ref[:, gl] = st * total[:, gl] + upd
            y = (y_intra + y_inter + xs[:, gl] * dexp_ref[:, gl]) * gate[:, gl]
            ms = jnp.mean(y * y, axis=-1, keepdims=True)
            o_ref[rows, gl] = (y * lax.rsqrt(ms + LN_EPS) * ngain_ref[:, gl]).astype(o_ref.dtype)


def _ssd(rest, dtr, convw, convb, dtb, a_log, dexp, ngain, expand, batch, seq):
    t = SSD_STEP
    nc = seq // t
    row = lambda b, c: b * nc + c
    const2 = lambda b, c: (0, 0)
    return pl.pallas_call(
        _ssd_kernel,
        grid=(batch, nc),
        in_specs=[
            pl.BlockSpec((t, SSD_WIDTH), lambda b, c: (row(b, c), 2)),
            pl.BlockSpec((t, CONV_CH), lambda b, c: (row(b, c), 2)),
            pl.BlockSpec((t, LANES), lambda b, c: (row(b, c), 0)),
            pl.BlockSpec((CONV_WIDTH, CONV_CH), const2),
            pl.BlockSpec((1, CONV_CH), const2),
            pl.BlockSpec((1, LANES), const2),
            pl.BlockSpec((1, LANES), const2),
            pl.BlockSpec((1, SSD_WIDTH), const2),
            pl.BlockSpec((1, SSD_WIDTH), const2),
            pl.BlockSpec((LANES, SSD_WIDTH), const2),
        ],
        out_specs=pl.BlockSpec((t, SSD_WIDTH), lambda b, c: (row(b, c), 0)),
        out_shape=jax.ShapeDtypeStruct((batch * seq, SSD_WIDTH), BF16),
        scratch_shapes=[
            pltpu.VMEM((t + SUBLANES, CONV_CH), F32),
            pltpu.VMEM((SSD_STATE, SSD_WIDTH), F32),
            pltpu.VMEM((t, CONV_CH), F32),
        ],
        compiler_params=_cparams(("parallel", "arbitrary")),
        name="ssd",
    )(rest, rest, dtr, convw, convb, dtb, a_log, dexp, ngain, expand)


def _outproj_ln_kernel(ret_ref, ssd_ref, x_ref, wo_ref, gain_ref, bias_ref, o_ref, *, alpha):
    mix = _dot(ret_ref[...], wo_ref[0:RET_WIDTH, :]) + _dot(ssd_ref[...], wo_ref[RET_WIDTH:, :])
    y = alpha * x_ref[...] + mix
    o_ref[...] = _layer_norm_rows(y, gain_ref[...], bias_ref[...])


def _outproj_ln(ret, ssd, x, wo, gain, bias, alpha):
    m, d = x.shape
    return pl.pallas_call(
        functools.partial(_outproj_ln_kernel, alpha=alpha),
        grid=(m // OUT_TM,),
        in_specs=[
            pl.BlockSpec((OUT_TM, RET_WIDTH), lambda i: (i, 0)),
            pl.BlockSpec((OUT_TM, SSD_WIDTH), lambda i: (i, 0)),
            pl.BlockSpec((OUT_TM, d), lambda i: (i, 0)),
            pl.BlockSpec((RET_WIDTH + SSD_WIDTH, d), lambda i: (0, 0)),
            pl.BlockSpec((1, d), lambda i: (0, 0)),
            pl.BlockSpec((1, d), lambda i: (0, 0)),
        ],
        out_specs=pl.BlockSpec((OUT_TM, d), lambda i: (i, 0)),
        out_shape=jax.ShapeDtypeStruct((m, d), F32),
        compiler_params=_cparams(("parallel",)),
        name="outproj_ln",
    )(ret, ssd, x, wo, gain, bias)


def _even_odd_head_columns(w):
    d, n = w.shape
    w4 = w.reshape(d, n // RET_HEAD_DIM, RET_HEAD_DIM // 2, 2)
    return jnp.swapaxes(w4, 2, 3).reshape(d, n)


def kernel(x, positions, ffn1_w_gate, ffn1_w_up, ffn1_w_down, ln1_gain, ln1_bias, mix_w_in, ret_gn_gain, ret_gn_bias, ssd_conv_w, ssd_conv_b, ssd_dt_bias, ssd_a_log, ssd_d, ssd_norm_gain, mix_w_out, ln2_gain, ln2_bias, ffn2_w_gate, ffn2_w_up, ffn2_w_down, ln3_gain, ln3_bias):
    batch, seq, d = x.shape
    depth = ffn1_w_gate.shape[0]
    alpha = (2.0 * depth) ** 0.25
    m = batch * seq
    h = x.reshape(m, d)

    half = RET_HEAD_DIM // 2
    inv_freq = (1.0 / (ROPE_BASE ** jnp.linspace(0.0, 1.0, half, dtype=F32))).reshape(1, half)
    cos, sin = _rope_tables(positions.reshape(m, 1), inv_freq)
    dmask, qd, kd = _retention_consts(RET_CHUNK)
    expand = (jnp.arange(SSD_WIDTH)[None, :] // SSD_HEAD_DIM == jnp.arange(LANES)[:, None]).astype(BF16)
    row = lambda v: v.reshape(1, -1).astype(F32)
    pad_lanes = lambda v: jnp.pad(v.astype(F32), (0, LANES - v.shape[0])).reshape(1, LANES)

    for layer in range(depth):
        w_in = mix_w_in[layer]
        rw = RET_WIDTH
        w_qk = _even_odd_head_columns(w_in[:, 0:2 * rw]).astype(BF16)
        n_rest = 2 * rw + SSD_WIDTH + CONV_CH
        w_rest = w_in[:, 2 * rw:2 * rw + n_rest].astype(BF16)
        w_dt = jnp.pad(w_in[:, 2 * rw + n_rest:], ((0, 0), (0, LANES - SSD_HEADS))).astype(BF16)

        h = _ffn_ln(h, ffn1_w_gate[layer].astype(BF16), ffn1_w_up[layer].astype(BF16),
                    ffn1_w_down[layer].astype(BF16), row(ln1_gain[layer]), row(ln1_bias[layer]), alpha)

        qk, rest, dtr = _in_proj(h, w_qk, w_rest, w_dt, cos, sin)

        ret = _retention(qk, rest, dmask, qd, kd, row(ret_gn_gain[layer]), row(ret_gn_bias[layer]),
                         batch, seq)
        dexp = jnp.repeat(ssd_d[layer].astype(F32), SSD_HEAD_DIM).reshape(1, SSD_WIDTH)
        ssd = _ssd(rest, dtr, ssd_conv_w[layer].astype(F32), row(ssd_conv_b[layer]),
                   pad_lanes(ssd_dt_bias[layer]), pad_lanes(ssd_a_log[layer]), dexp, row(ssd_norm_gain[layer]), expand,
                   batch, seq)

        h = _outproj_ln(ret, ssd, h, mix_w_out[layer].astype(BF16), row(ln2_gain[layer]),
                        row(ln2_bias[layer]), alpha)

        h = _ffn_ln(h, ffn2_w_gate[layer].astype(BF16), ffn2_w_up[layer].astype(BF16),
                    ffn2_w_down[layer].astype(BF16), row(ln3_gain[layer]), row(ln3_bias[layer]), alpha)

    return h.reshape(batch, seq, d)
```

```python
import functools
import math

import jax
import jax.numpy as jnp
import numpy as np
from jax import lax
from jax.experimental import pallas as pl
from jax.experimental.pallas import tpu as pltpu

F32 = jnp.float32
BF16 = jnp.bfloat16

D_MODEL = 2048
RET_WIDTH = 1024
RET_HEAD_DIM = 256
RET_HEADS = 4
SSD_WIDTH = 1024
SSD_HEAD_DIM = 64
SSD_HEADS = 16
SSD_GROUPS = 2
SSD_STATE = 128
CONV_WIDTH = 4
CONV_CH = SSD_WIDTH + 2 * SSD_GROUPS * SSD_STATE
D_FF = 5632
ROPE_BASE = 10000.0
LN_EPS = 1e-5
FFN_RES_WEIGHT = 0.5

LANES = 128
SUBLANES = 8
VMEM_BYTES_V7X = 64 * 1024 * 1024
VMEM_LIMIT = VMEM_BYTES_V7X - 8 * 1024 * 1024
VMEM_LIMIT_FFN = VMEM_BYTES_V7X - 1024 * 1024

FFN_TM = 1024
FFN_TF = 512
PROJ_TM = 1024
PROJ_TN_QK = 512
PROJ_TN_REST = 768
OUT_TM = 512
RET_CHUNK = 256
SSD_CHUNK = 128
SSD_STEP = 256
ROPE_TM = 1024


def _cparams(sem, vmem_limit=VMEM_LIMIT):
    return pltpu.CompilerParams(dimension_semantics=sem, vmem_limit_bytes=vmem_limit)


def _silu(v):
    return v / (1.0 + jnp.exp(-v))


def _layer_norm_rows(y, gain, bias):
    mu = jnp.mean(y, axis=-1, keepdims=True)
    d = y - mu
    var = jnp.mean(d * d, axis=-1, keepdims=True)
    return d * lax.rsqrt(var + LN_EPS) * gain + bias


def _split3(v):
    h0 = v.astype(BF16)
    r1 = v - h0.astype(F32)
    h1 = r1.astype(BF16)
    h2 = (r1 - h1.astype(F32)).astype(BF16)
    return h0, h1, h2


def _dot(a, b):
    return jnp.dot(a, b, preferred_element_type=F32)


def _dot_exact01(m01, v, left=True):
    parts = _split3(v)
    if left:
        return _dot(m01, parts[0]) + _dot(m01, parts[1]) + _dot(m01, parts[2])
    return _dot(parts[0], m01) + _dot(parts[1], m01) + _dot(parts[2], m01)


def _ffn_ln_kernel(x_ref, wg_ref, wu_ref, wd_ref, gain_ref, bias_ref, o_ref, xb_ref, *, alpha, n_f):
    f = pl.program_id(1)

    @pl.when(f == 0)
    def _():
        xb_ref[...] = x_ref[...].astype(BF16)
        o_ref[...] = jnp.zeros_like(o_ref)

    half = xb_ref.shape[0] // 2
    for r in range(2):
        rows = slice(r * half, (r + 1) * half)
        xb = xb_ref[rows, :]
        h = (_silu(_dot(xb, wg_ref[...])) * _dot(xb, wu_ref[...])).astype(BF16)
        o_ref[rows, :] += _dot(h, wd_ref[...])

    @pl.when(f == n_f - 1)
    def _():
        y = alpha * x_ref[...] + FFN_RES_WEIGHT * o_ref[...]
        o_ref[...] = _layer_norm_rows(y, gain_ref[...], bias_ref[...])


def _ffn_ln(x, wg, wu, wd, gain, bias, alpha):
    m, d = x.shape
    n_f = D_FF // FFN_TF
    return pl.pallas_call(
        functools.partial(_ffn_ln_kernel, alpha=alpha, n_f=n_f),
        grid=(m // FFN_TM, n_f),
        in_specs=[
            pl.BlockSpec((FFN_TM, d), lambda i, f: (i, 0)),
            pl.BlockSpec((d, FFN_TF), lambda i, f: (0, f)),
            pl.BlockSpec((d, FFN_TF), lambda i, f: (0, f)),
            pl.BlockSpec((FFN_TF, d), lambda i, f: (f, 0)),
            pl.BlockSpec((1, d), lambda i, f: (0, 0)),
            pl.BlockSpec((1, d), lambda i, f: (0, 0)),
        ],
        out_specs=pl.BlockSpec((FFN_TM, d), lambda i, f: (i, 0)),
        out_shape=jax.ShapeDtypeStruct((m, d), F32),
        scratch_shapes=[pltpu.VMEM((FFN_TM, d), BF16)],
        compiler_params=_cparams(("parallel", "arbitrary"), VMEM_LIMIT_FFN),
        name="ffn_ln",
    )(x, wg, wu, wd, gain, bias)


def _rope_tab_kernel(pos_ref, invf_ref, cos_ref, sin_ref):
    theta = pos_ref[...].astype(F32) * invf_ref[...]
    cos_ref[...] = jnp.cos(theta)
    sin_ref[...] = jnp.sin(theta)


def _rope_tables(pos_col, inv_freq):
    m = pos_col.shape[0]
    half = inv_freq.shape[1]
    return pl.pallas_call(
        _rope_tab_kernel,
        grid=(m // ROPE_TM,),
        in_specs=[
            pl.BlockSpec((ROPE_TM, 1), lambda i: (i, 0)),
            pl.BlockSpec((1, half), lambda i: (0, 0)),
        ],
        out_specs=[
            pl.BlockSpec((ROPE_TM, half), lambda i: (i, 0)),
            pl.BlockSpec((ROPE_TM, half), lambda i: (i, 0)),
        ],
        out_shape=[jax.ShapeDtypeStruct((m, half), F32)] * 2,
        compiler_params=_cparams(("parallel",)),
        name="rope_tab",
    )(pos_col, inv_freq)


def _in_proj_kernel(x_ref, wqk_ref, wrest_ref, wdt_ref, cos_ref, sin_ref, qk_ref, rest_ref, dt_ref, xb_ref,
                    *, n_qk, k_scale):
    j = pl.program_id(1)

    @pl.when(j == 0)
    def _():
        xb_ref[...] = x_ref[...].astype(BF16)
        dt_ref[...] = _dot(xb_ref[...], wdt_ref[...])

    @pl.when(j < n_qk)
    def _():
        acc = _dot(xb_ref[...], wqk_ref[...])
        scale = jnp.where(j >= n_qk // 2, k_scale, 1.0).astype(F32)
        c = cos_ref[...] * scale
        s = sin_ref[...] * scale
        half = RET_HEAD_DIM // 2
        for hd in range(acc.shape[1] // RET_HEAD_DIM):
            lo = hd * RET_HEAD_DIM
            e = acc[:, lo:lo + half]
            o = acc[:, lo + half:lo + RET_HEAD_DIM]
            qk_ref[:, lo:lo + half] = (e * c - o * s).astype(qk_ref.dtype)
            qk_ref[:, lo + half:lo + RET_HEAD_DIM] = (o * c + e * s).astype(qk_ref.dtype)

    @pl.when(j >= n_qk)
    def _():
        rest_ref[...] = _dot(xb_ref[...], wrest_ref[...]).astype(rest_ref.dtype)


def _in_proj(x, w_qk, w_rest, w_dt, cos, sin):
    m, k = x.shape
    half = RET_HEAD_DIM // 2
    n_qk = w_qk.shape[1] // PROJ_TN_QK
    n_rest = w_rest.shape[1] // PROJ_TN_REST
    qk_idx = lambda i, j: (0, jnp.minimum(j, n_qk - 1))
    rest_idx = lambda i, j: (0, jnp.maximum(j - n_qk, 0))
    return pl.pallas_call(
        functools.partial(_in_proj_kernel, n_qk=n_qk, k_scale=RET_HEAD_DIM ** -0.5),
        grid=(m // PROJ_TM, n_qk + n_rest),
        in_specs=[
            pl.BlockSpec((PROJ_TM, k), lambda i, j: (i, 0)),
            pl.BlockSpec((k, PROJ_TN_QK), qk_idx),
            pl.BlockSpec((k, PROJ_TN_REST), rest_idx),
            pl.BlockSpec((k, LANES), lambda i, j: (0, 0)),
            pl.BlockSpec((PROJ_TM, half), lambda i, j: (i, 0)),
            pl.BlockSpec((PROJ_TM, half), lambda i, j: (i, 0)),
        ],
        out_specs=[
            pl.BlockSpec((PROJ_TM, PROJ_TN_QK), lambda i, j: (i, qk_idx(i, j)[1])),
            pl.BlockSpec((PROJ_TM, PROJ_TN_REST), lambda i, j: (i, rest_idx(i, j)[1])),
            pl.BlockSpec((PROJ_TM, LANES), lambda i, j: (i, 0)),
        ],
        out_shape=[
            jax.ShapeDtypeStruct((m, w_qk.shape[1]), BF16),
            jax.ShapeDtypeStruct((m, w_rest.shape[1]), BF16),
            jax.ShapeDtypeStruct((m, LANES), F32),
        ],
        scratch_shapes=[pltpu.VMEM((PROJ_TM, k), BF16)],
        compiler_params=_cparams(("parallel", "arbitrary")),
        name="in_proj",
    )(x, w_qk, w_rest, w_dt, cos, sin)


def _ret_log_gamma(h):
    return math.log(1.0 - 2.0 ** (-5.0 - h))


def _retention_kernel(q_ref, k_ref, v_ref, g_ref, dmask_ref, qd_ref, kd_ref, gain_ref, bias_ref,
                      o_ref, state_ref):
    @pl.when(pl.program_id(1) == 0)
    def _():
        state_ref[...] = jnp.zeros_like(state_ref)

    tc = q_ref.shape[0]
    dh = RET_HEAD_DIM
    for h in range(RET_HEADS):
        sl = slice(h * dh, (h + 1) * dh)
        q = q_ref[:, sl]
        k = k_ref[:, sl]
        v = v_ref[:, sl]
        scores = lax.dot_general(q, k, (((1,), (1,)), ((), ())), preferred_element_type=F32)
        p = (scores * dmask_ref[h]).astype(BF16)
        state = state_ref[h]
        o = _dot(p, v) + _dot(q, state.astype(BF16)) * qd_ref[:, sl]
        vkd = (v.astype(F32) * kd_ref[:, sl]).astype(BF16)
        kv = lax.dot_general(k, vkd, (((0,), (0,)), ((), ())), preferred_element_type=F32)
        state_ref[h] = state * math.exp(_ret_log_gamma(h) * tc) + kv
        mu = jnp.mean(o, axis=-1, keepdims=True)
        d = o - mu
        var = jnp.mean(d * d, axis=-1, keepdims=True)
        on = d * lax.rsqrt(var + LN_EPS) * gain_ref[:, sl] + bias_ref[:, sl]
        o_ref[:, sl] = (_silu(g_ref[:, sl].astype(F32)) * on).astype(o_ref.dtype)


def _retention(qk, rest, dmask, qd, kd, gain, bias, batch, seq):
    tc = RET_CHUNK
    nc = seq // tc
    w = RET_WIDTH
    row = lambda b, c: b * nc + c
    const2 = lambda b, c: (0, 0)
    return pl.pallas_call(
        _retention_kernel,
        grid=(batch, nc),
        in_specs=[
            pl.BlockSpec((tc, w), lambda b, c: (row(b, c), 0)),
            pl.BlockSpec((tc, w), lambda b, c: (row(b, c), 1)),
            pl.BlockSpec((tc, w), lambda b, c: (row(b, c), 0)),
            pl.BlockSpec((tc, w), lambda b, c: (row(b, c), 1)),
            pl.BlockSpec((RET_HEADS, tc, tc), lambda b, c: (0, 0, 0)),
            pl.BlockSpec((tc, w), const2),
            pl.BlockSpec((tc, w), const2),
            pl.BlockSpec((1, w), const2),
            pl.BlockSpec((1, w), const2),
        ],
        out_specs=pl.BlockSpec((tc, w), lambda b, c: (row(b, c), 0)),
        out_shape=jax.ShapeDtypeStruct((batch * seq, w), BF16),
        scratch_shapes=[pltpu.VMEM((RET_HEADS, RET_HEAD_DIM, RET_HEAD_DIM), F32)],
        compiler_params=_cparams(("parallel", "arbitrary")),
        name="retention",
    )(qk, qk, rest, rest, dmask, qd, kd, gain, bias)


def _retention_consts(tc):
    lg = jnp.asarray([_ret_log_gamma(h) for h in range(RET_HEADS)], F32)
    pos = jnp.arange(tc, dtype=F32)
    rel = pos[:, None] - pos[None, :]
    dmask = jnp.where(rel >= 0, jnp.exp(lg[:, None, None] * jnp.maximum(rel, 0.0)), 0.0)
    qd = jnp.exp(lg[None, :] * (pos[:, None] + 1.0))
    kd = jnp.exp(lg[None, :] * (tc - 1.0 - pos[:, None]))
    rep = lambda t: jnp.repeat(t, RET_HEAD_DIM, axis=1)
    return dmask.astype(F32), rep(qd).astype(F32), rep(kd).astype(F32)


def _ssd_kernel(z_ref, xbc_ref, dtr_ref, convw_ref, convb_ref, dtb_ref, alog_ref, dexp_ref, ngain_ref,
                expand_ref, o_ref, xpad_ref, state_ref, xc_ref):
    @pl.when(pl.program_id(1) == 0)
    def _():
        xpad_ref[0:SUBLANES, :] = jnp.zeros((SUBLANES, CONV_CH), F32)
        state_ref[...] = jnp.zeros_like(state_ref)

    t = xbc_ref.shape[0]
    ts = SSD_CHUNK
    n = SSD_STATE
    gw = SSD_WIDTH // SSD_GROUPS

    xf = xbc_ref[...].astype(F32)
    xpad_ref[SUBLANES:, :] = xf
    conv = convb_ref[...] + convw_ref[CONV_WIDTH - 1:CONV_WIDTH, :] * xf
    for s in range(1, CONV_WIDTH):
        w_s = convw_ref[CONV_WIDTH - 1 - s:CONV_WIDTH - s, :]
        conv = conv + w_s * xpad_ref[SUBLANES - s:SUBLANES - s + t, :]
    xc_ref[...] = _silu(conv)
    xpad_ref[0:SUBLANES, :] = xf[t - SUBLANES:t, :]

    ri = lax.broadcasted_iota(jnp.int32, (ts, ts), 0)
    ci = lax.broadcasted_iota(jnp.int32, (ts, ts), 1)
    causal = ri >= ci
    tri = jnp.where(causal, 1.0, 0.0).astype(BF16)
    lane_lo = ci < SSD_HEAD_DIM
    expand = expand_ref[...]
    a_neg = -jnp.exp(alog_ref[...])

    for j in range(t // ts):
        rows = slice(j * ts, (j + 1) * ts)
        xs = xc_ref[rows, 0:SSD_WIDTH]
        bm = xc_ref[rows, SSD_WIDTH:SSD_WIDTH + SSD_GROUPS * n].astype(BF16)
        cm = xc_ref[rows, SSD_WIDTH + SSD_GROUPS * n:CONV_CH].astype(BF16)

        pre = dtr_ref[rows, :] + dtb_ref[...]
        dt = jnp.maximum(pre, 0.0) + jnp.log1p(jnp.exp(-jnp.abs(pre)))
        da = dt * a_neg
        acum = _dot_exact01(tri, da, left=True)
        acum_t = acum.T
        acum_e = _dot_exact01(expand, acum, left=False)
        dt_e = _dot_exact01(expand, dt, left=False)
        last_e = acum_e[ts - 1:ts, :]
        from_start = jnp.exp(acum_e)
        to_end = jnp.exp(last_e - acum_e)
        total = jnp.exp(last_e)

        xdt = xs * dt_e
        xdt_te = (xdt * to_end).astype(BF16)

        zf = z_ref[rows, :].astype(F32)
        gate = _silu(zf)
        for g in range(SSD_GROUPS):
            cg = cm[:, g * n:(g + 1) * n]
            bg = bm[:, g * n:(g + 1) * n]
            cb = lax.dot_general(cg, bg, (((1,), (1,)), ((), ())), preferred_element_type=F32)
            intra = []
            for pr in range(gw // LANES):
                lanes = slice(g * gw + pr * LANES, g * gw + (pr + 1) * LANES)
                xpair = xdt[:, lanes]
                y_pair = None
                for half in range(2):
                    hd = (g * gw + pr * LANES) // SSD_HEAD_DIM + half
                    seg = acum[:, hd:hd + 1] - acum_t[hd:hd + 1, :]
                    decay = jnp.exp(jnp.where(causal, seg, -1e30))
                    wmat = (cb * decay).astype(BF16)
                    keep = lane_lo if half == 0 else jnp.logical_not(lane_lo)
                    xh = jnp.where(keep, xpair, 0.0).astype(BF16)
                    term = _dot(wmat, xh)
                    y_pair = term if y_pair is None else y_pair + term
                intra.append(y_pair)
            y_intra = jnp.concatenate(intra, axis=1)
            gl = slice(g * gw, (g + 1) * gw)
            st = state_ref[:, gl]
            y_inter = _dot(cg, st.astype(BF16)) * from_start[:, gl]
            upd = lax.dot_general(bg, xdt_te[:, gl], (((0,), (0,)), ((), ())), preferred_element_type=F32)
            state_ref[:, gl] = st * total[:, gl] + upd
            y = (y_intra + y_inter + xs[:, gl] * dexp_ref[:, gl]) * gate[:, gl]
            ms = jnp.mean(y * y, axis=-1, keepdims=True)
            o_ref[rows, gl] = (y * lax.rsqrt(ms + LN_EPS) * ngain_ref[:, gl]).astype(o_ref.dtype)


def _ssd(rest, dtr, convw, convb, dtb, a_log, dexp, ngain, expand, batch, seq):
    t = SSD_STEP
    nc = seq // t
    row = lambda b, c: b * nc + c
    const2 = lambda b, c: (0, 0)
    return pl.pallas_call(
        _ssd_kernel,
        grid=(batch, nc),
        in_specs=[
            pl.BlockSpec((t, SSD_WIDTH), lambda b, c: (row(b, c), 2)),
            pl.BlockSpec((t, CONV_CH), lambda b, c: (row(b, c), 2)),
            pl.BlockSpec((t, LANES), lambda b, c: (row(b, c), 0)),
            pl.BlockSpec((CONV_WIDTH, CONV_CH), const2),
            pl.BlockSpec((1, CONV_CH), const2),
            pl.BlockSpec((1, LANES), const2),
            pl.BlockSpec((1, LANES), const2),
            pl.BlockSpec((1, SSD_WIDTH), const2),
            pl.BlockSpec((1, SSD_WIDTH), const2),
            pl.BlockSpec((LANES, SSD_WIDTH), const2),
        ],
        out_specs=pl.BlockSpec((t, SSD_WIDTH), lambda b, c: (row(b, c), 0)),
        out_shape=jax.ShapeDtypeStruct((batch * seq, SSD_WIDTH), BF16),
        scratch_shapes=[
            pltpu.VMEM((t + SUBLANES, CONV_CH), F32),
            pltpu.VMEM((SSD_STATE, SSD_WIDTH), F32),
            pltpu.VMEM((t, CONV_CH), F32),
        ],
        compiler_params=_cparams(("parallel", "arbitrary")),
        name="ssd",
    )(rest, rest, dtr, convw, convb, dtb, a_log, dexp, ngain, expand)


def _outproj_ln_kernel(ret_ref, ssd_ref, x_ref, wo_ref, gain_ref, bias_ref, o_ref, *, alpha):
    mix = _dot(ret_ref[...], wo_ref[0:RET_WIDTH, :]) + _dot(ssd_ref[...], wo_ref[RET_WIDTH:, :])
    y = alpha * x_ref[...] + mix
    o_ref[...] = _layer_norm_rows(y, gain_ref[...], bias_ref[...])


def _outproj_ln(ret, ssd, x, wo, gain, bias, alpha):
    m, d = x.shape
    return pl.pallas_call(
        functools.partial(_outproj_ln_kernel, alpha=alpha),
        grid=(m // OUT_TM,),
        in_specs=[
            pl.BlockSpec((OUT_TM, RET_WIDTH), lambda i: (i, 0)),
            pl.BlockSpec((OUT_TM, SSD_WIDTH), lambda i: (i, 0)),
            pl.BlockSpec((OUT_TM, d), lambda i: (i, 0)),
            pl.BlockSpec((RET_WIDTH + SSD_WIDTH, d), lambda i: (0, 0)),
            pl.BlockSpec((1, d), lambda i: (0, 0)),
            pl.BlockSpec((1, d), lambda i: (0, 0)),
        ],
        out_specs=pl.BlockSpec((OUT_TM, d), lambda i: (i, 0)),
        out_shape=jax.ShapeDtypeStruct((m, d), F32),
        compiler_params=_cparams(("parallel",)),
        name="outproj_ln",
    )(ret, ssd, x, wo, gain, bias)


def _even_odd_head_columns(w):
    d, n = w.shape
    w4 = w.reshape(d, n // RET_HEAD_DIM, RET_HEAD_DIM // 2, 2)
    return jnp.swapaxes(w4, 2, 3).reshape(d, n)


def kernel(x, positions, ffn1_w_gate, ffn1_w_up, ffn1_w_down, ln1_gain, ln1_bias, mix_w_in, ret_gn_gain, ret_gn_bias, ssd_conv_w, ssd_conv_b, ssd_dt_bias, ssd_a_log, ssd_d, ssd_norm_gain, mix_w_out, ln2_gain, ln2_bias, ffn2_w_gate, ffn2_w_up, ffn2_w_down, ln3_gain, ln3_bias):
    batch, seq, d = x.shape
    depth = ffn1_w_gate.shape[0]
    alpha = (2.0 * depth) ** 0.25
    m = batch * seq
    h = x.reshape(m, d)

    half = RET_HEAD_DIM // 2
    inv_freq = (1.0 / (ROPE_BASE ** jnp.linspace(0.0, 1.0, half, dtype=F32))).reshape(1, half)
    cos, sin = _rope_tables(positions.reshape(m, 1), inv_freq)
    dmask, qd, kd = _retention_consts(RET_CHUNK)
    expand = (jnp.arange(SSD_WIDTH)[None, :] // SSD_HEAD_DIM == jnp.arange(LANES)[:, None]).astype(BF16)
    row = lambda v: v.reshape(1, -1).astype(F32)
    pad_lanes = lambda v: jnp.pad(v.astype(F32), (0, LANES - v.shape[0])).reshape(1, LANES)

    for layer in range(depth):
        w_in = mix_w_in[layer]
        rw = RET_WIDTH
        w_qk = _even_odd_head_columns(w_in[:, 0:2 * rw]).astype(BF16)
        n_rest = 2 * rw + SSD_WIDTH + CONV_CH
        w_rest = w_in[:, 2 * rw:2 * rw + n_rest].astype(BF16)
        w_dt = jnp.pad(w_in[:, 2 * rw + n_rest:], ((0, 0), (0, LANES - SSD_HEADS))).astype(BF16)

        h = _ffn_ln(h, ffn1_w_gate[layer].astype(BF16), ffn1_w_up[layer].astype(BF16),
                    ffn1_w_down[layer].astype(BF16), row(ln1_gain[layer]), row(ln1_bias[layer]), alpha)

        qk, rest, dtr = _in_proj(h, w_qk, w_rest, w_dt, cos, sin)

        ret = _retention(qk, rest, dmask, qd, kd, row(ret_gn_gain[layer]), row(ret_gn_bias[layer]),
                         batch, seq)
        dexp = jnp.repeat(ssd_d[layer].astype(F32), SSD_HEAD_DIM).reshape(1, SSD_WIDTH)
        ssd = _ssd(rest, dtr, ssd_conv_w[layer].astype(F32), row(ssd_conv_b[layer]),
                   pad_lanes(ssd_dt_bias[layer]), pad_lanes(ssd_a_log[layer]), dexp, row(ssd_norm_gain[layer]), expand,
                   batch, seq)

        h = _outproj_ln(ret, ssd, h, mix_w_out[layer].astype(BF16), row(ln2_gain[layer]),
                        row(ln2_bias[layer]), alpha)

        h = _ffn_ln(h, ffn2_w_gate[layer].astype(BF16), ffn2_w_up[layer].astype(BF16),
                    ffn2_w_down[layer].astype(BF16), row(ln3_gain[layer]), row(ln3_bias[layer]), alpha)

    return h.reshape(batch, seq, d)
```

```python
import functools
import math

import jax
import jax.numpy as jnp
import numpy as np
from jax import lax
from jax.experimental import pallas as pl
from jax.experimental.pallas import tpu as pltpu

F32 = jnp.float32
BF16 = jnp.bfloat16

D_MODEL = 2048
RET_WIDTH = 1024
RET_HEAD_DIM = 256
RET_HEADS = 4
SSD_WIDTH = 1024
SSD_HEAD_DIM = 64
SSD_HEADS = 16
SSD_GROUPS = 2
SSD_STATE = 128
CONV_WIDTH = 4
CONV_CH = SSD_WIDTH + 2 * SSD_GROUPS * SSD_STATE
D_FF = 5632
ROPE_BASE = 10000.0
LN_EPS = 1e-5
FFN_RES_WEIGHT = 0.5

LANES = 128
SUBLANES = 8
VMEM_BYTES_V7X = 64 * 1024 * 1024
VMEM_LIMIT = VMEM_BYTES_V7X - 4 * 1024 * 1024
VMEM_LIMIT_FFN = VMEM_BYTES_V7X - 1024 * 1024

FFN_TM = 1024
FFN_TF = 512
FFN_ROWS = 512
FFN_LN_ROWS = 256
PROJ_TM = 1024
PROJ_TN_QK = 512
PROJ_TN_REST = 768
OUT_TM = 1024
OUT_ROWS = 128
RET_CHUNK = 256
SSD_CHUNK = 128
SSD_STEP = 256
ROPE_TM = 1024


def _cparams(sem, vmem_limit=VMEM_LIMIT):
    return pltpu.CompilerParams(dimension_semantics=sem, vmem_limit_bytes=vmem_limit)


def _silu(v):
    return v / (1.0 + jnp.exp(-v))


def _layer_norm_rows(y, gain, bias):
    mu = jnp.mean(y, axis=-1, keepdims=True)
    d = y - mu
    var = jnp.mean(d * d, axis=-1, keepdims=True)
    return d * lax.rsqrt(var + LN_EPS) * gain + bias


def _split3(v):
    h0 = v.astype(BF16)
    r1 = v - h0.astype(F32)
    h1 = r1.astype(BF16)
    h2 = (r1 - h1.astype(F32)).astype(BF16)
    return h0, h1, h2


def _dot(a, b):
    return jnp.dot(a, b, preferred_element_type=F32)


def _dot_exact01(m01, v, left=True):
    parts = _split3(v)
    if left:
        return _dot(m01, parts[0]) + _dot(m01, parts[1]) + _dot(m01, parts[2])
    return _dot(parts[0], m01) + _dot(parts[1], m01) + _dot(parts[2], m01)


def _ffn_ln_kernel(x_ref, wg_ref, wu_ref, wd_ref, gain_ref, bias_ref, o_ref, xb_ref, *, alpha, n_f):
    f = pl.program_id(1)

    @pl.when(f == 0)
    def _():
        xb_ref[...] = x_ref[...].astype(BF16)
        o_ref[...] = jnp.zeros_like(o_ref)

    def partial_out(rows):
        xb = xb_ref[rows, :]
        h = (_silu(_dot(xb, wg_ref[...])) * _dot(xb, wu_ref[...])).astype(BF16)
        return _dot(h, wd_ref[...])

    @pl.when(f < n_f - 1)
    def _():
        for r in range(FFN_TM // FFN_ROWS):
            rows = slice(r * FFN_ROWS, (r + 1) * FFN_ROWS)
            o_ref[rows, :] += partial_out(rows)

    @pl.when(f == n_f - 1)
    def _():
        for r in range(FFN_TM // FFN_LN_ROWS):
            rows = slice(r * FFN_LN_ROWS, (r + 1) * FFN_LN_ROWS)
            y = alpha * x_ref[rows, :] + FFN_RES_WEIGHT * (o_ref[rows, :] + partial_out(rows))
            o_ref[rows, :] = _layer_norm_rows(y, gain_ref[...], bias_ref[...])


def _ffn_ln(x, wg, wu, wd, gain, bias, alpha):
    m, d = x.shape
    n_f = wg.shape[0]
    return pl.pallas_call(
        functools.partial(_ffn_ln_kernel, alpha=alpha, n_f=n_f),
        grid=(m // FFN_TM, n_f),
        in_specs=[
            pl.BlockSpec((FFN_TM, d), lambda i, f: (i, 0)),
            pl.BlockSpec((None, d, FFN_TF), lambda i, f: (f, 0, 0)),
            pl.BlockSpec((None, d, FFN_TF), lambda i, f: (f, 0, 0)),
            pl.BlockSpec((FFN_TF, d), lambda i, f: (f, 0)),
            pl.BlockSpec((1, d), lambda i, f: (0, 0)),
            pl.BlockSpec((1, d), lambda i, f: (0, 0)),
        ],
        out_specs=pl.BlockSpec((FFN_TM, d), lambda i, f: (i, 0)),
        out_shape=jax.ShapeDtypeStruct((m, d), F32),
        scratch_shapes=[pltpu.VMEM((FFN_TM, d), BF16)],
        compiler_params=_cparams(("parallel", "arbitrary"), VMEM_LIMIT_FFN),
        name="ffn_ln",
    )(x, wg, wu, wd, gain, bias)


def _rope_tab_kernel(pos_ref, invf_ref, cos_ref, sin_ref):
    theta = pos_ref[...].astype(F32) * invf_ref[...]
    cos_ref[...] = jnp.cos(theta)
    sin_ref[...] = jnp.sin(theta)


def _rope_tables(pos_col, inv_freq):
    m = pos_col.shape[0]
    half = inv_freq.shape[1]
    return pl.pallas_call(
        _rope_tab_kernel,
        grid=(m // ROPE_TM,),
        in_specs=[
            pl.BlockSpec((ROPE_TM, 1), lambda i: (i, 0)),
            pl.BlockSpec((1, half), lambda i: (0, 0)),
        ],
        out_specs=[
            pl.BlockSpec((ROPE_TM, half), lambda i: (i, 0)),
            pl.BlockSpec((ROPE_TM, half), lambda i: (i, 0)),
        ],
        out_shape=[jax.ShapeDtypeStruct((m, half), F32)] * 2,
        compiler_params=_cparams(("parallel",)),
        name="rope_tab",
    )(pos_col, inv_freq)


def _in_proj_kernel(x_ref, wqk_ref, wrest_ref, wdt_ref, cos_ref, sin_ref, qk_ref, rest_ref, dt_ref, xb_ref,
                    *, n_qk, k_scale):
    j = pl.program_id(1)

    @pl.when(j == 0)
    def _():
        xb_ref[...] = x_ref[...].astype(BF16)
        dt_ref[...] = _dot(xb_ref[...], wdt_ref[...])

    @pl.when(j < n_qk)
    def _():
        acc = _dot(xb_ref[...], wqk_ref[...])
        scale = jnp.where(j >= n_qk // 2, k_scale, 1.0).astype(F32)
        c = cos_ref[...] * scale
        s = sin_ref[...] * scale
        half = RET_HEAD_DIM // 2
        for hd in range(acc.shape[1] // RET_HEAD_DIM):
            lo = hd * RET_HEAD_DIM
            e = acc[:, lo:lo + half]
            o = acc[:, lo + half:lo + RET_HEAD_DIM]
            qk_ref[:, lo:lo + half] = (e * c - o * s).astype(qk_ref.dtype)
            qk_ref[:, lo + half:lo + RET_HEAD_DIM] = (o * c + e * s).astype(qk_ref.dtype)

    @pl.when(j >= n_qk)
    def _():
        rest_ref[...] = _dot(xb_ref[...], wrest_ref[...]).astype(rest_ref.dtype)


def _in_proj(x, w_qk, w_rest, w_dt, cos, sin):
    m, k = x.shape
    half = RET_HEAD_DIM // 2
    n_qk, _, tn_qk = w_qk.shape
    n_rest, _, tn_rest = w_rest.shape
    qk_j = lambda j: jnp.minimum(j, n_qk - 1)
    rest_j = lambda j: jnp.maximum(j - n_qk, 0)
    return pl.pallas_call(
        functools.partial(_in_proj_kernel, n_qk=n_qk, k_scale=RET_HEAD_DIM ** -0.5),
        grid=(m // PROJ_TM, n_qk + n_rest),
        in_specs=[
            pl.BlockSpec((PROJ_TM, k), lambda i, j: (i, 0)),
            pl.BlockSpec((None, k, tn_qk), lambda i, j: (qk_j(j), 0, 0)),
            pl.BlockSpec((None, k, tn_rest), lambda i, j: (rest_j(j), 0, 0)),
            pl.BlockSpec((k, LANES), lambda i, j: (0, 0)),
            pl.BlockSpec((PROJ_TM, half), lambda i, j: (i, 0)),
            pl.BlockSpec((PROJ_TM, half), lambda i, j: (i, 0)),
        ],
        out_specs=[
            pl.BlockSpec((PROJ_TM, tn_qk), lambda i, j: (i, qk_j(j))),
            pl.BlockSpec((PROJ_TM, tn_rest), lambda i, j: (i, rest_j(j))),
            pl.BlockSpec((PROJ_TM, LANES), lambda i, j: (i, 0)),
        ],
        out_shape=[
            jax.ShapeDtypeStruct((m, n_qk * tn_qk), BF16),
            jax.ShapeDtypeStruct((m, n_rest * tn_rest), BF16),
            jax.ShapeDtypeStruct((m, LANES), F32),
        ],
        scratch_shapes=[pltpu.VMEM((PROJ_TM, k), BF16)],
        compiler_params=_cparams(("parallel", "arbitrary")),
        name="in_proj",
    )(x, w_qk, w_rest, w_dt, cos, sin)


def _ret_log_gamma(h):
    return math.log(1.0 - 2.0 ** (-5.0 - h))


def _retention_kernel(q_ref, k_ref, v_ref, g_ref, dmask_ref, qd_ref, kd_ref, gain_ref, bias_ref,
                      o_ref, state_ref):
    @pl.when(pl.program_id(1) == 0)
    def _():
        state_ref[...] = jnp.zeros_like(state_ref)

    tc = q_ref.shape[0]
    dh = RET_HEAD_DIM
    for h in range(RET_HEADS):
        sl = slice(h * dh, (h + 1) * dh)
        q = q_ref[:, sl]
        k = k_ref[:, sl]
        v = v_ref[:, sl]
        scores = lax.dot_general(q, k, (((1,), (1,)), ((), ())), preferred_element_type=F32)
        p = (scores * dmask_ref[h]).astype(BF16)
        state = state_ref[h]
        o = _dot(p, v) + _dot(q, state.astype(BF16)) * qd_ref[:, sl]
        vkd = (v.astype(F32) * kd_ref[:, sl]).astype(BF16)
        kv = lax.dot_general(k, vkd, (((0,), (0,)), ((), ())), preferred_element_type=F32)
        state_ref[h] = state * math.exp(_ret_log_gamma(h) * tc) + kv
        mu = jnp.mean(o, axis=-1, keepdims=True)
        d = o - mu
        var = jnp.mean(d * d, axis=-1, keepdims=True)
        on = d * lax.rsqrt(var + LN_EPS) * gain_ref[:, sl] + bias_ref[:, sl]
        o_ref[:, sl] = (_silu(g_ref[:, sl].astype(F32)) * on).astype(o_ref.dtype)


def _retention(qk, rest, dmask, qd, kd, gain, bias, batch, seq):
    tc = RET_CHUNK
    nc = seq // tc
    w = RET_WIDTH
    row = lambda b, c: b * nc + c
    const2 = lambda b, c: (0, 0)
    return pl.pallas_call(
        _retention_kernel,
        grid=(batch, nc),
        in_specs=[
            pl.BlockSpec((tc, w), lambda b, c: (row(b, c), 0)),
            pl.BlockSpec((tc, w), lambda b, c: (row(b, c), 1)),
            pl.BlockSpec((tc, w), lambda b, c: (row(b, c), 0)),
            pl.BlockSpec((tc, w), lambda b, c: (row(b, c), 1)),
            pl.BlockSpec((RET_HEADS, tc, tc), lambda b, c: (0, 0, 0)),
            pl.BlockSpec((tc, w), const2),
            pl.BlockSpec((tc, w), const2),
            pl.BlockSpec((1, w), const2),
            pl.BlockSpec((1, w), const2),
        ],
        out_specs=pl.BlockSpec((tc, w), lambda b, c: (row(b, c), 0)),
        out_shape=jax.ShapeDtypeStruct((batch * seq, w), BF16),
        scratch_shapes=[pltpu.VMEM((RET_HEADS, RET_HEAD_DIM, RET_HEAD_DIM), F32)],
        compiler_params=_cparams(("parallel", "arbitrary")),
        name="retention",
    )(qk, qk, rest, rest, dmask, qd, kd, gain, bias)


def _retention_consts(tc):
    lg = jnp.asarray([_ret_log_gamma(h) for h in range(RET_HEADS)], F32)
    pos = jnp.arange(tc, dtype=F32)
    rel = pos[:, None] - pos[None, :]
    dmask = jnp.where(rel >= 0, jnp.exp(lg[:, None, None] * jnp.maximum(rel, 0.0)), 0.0)
    qd = jnp.exp(lg[None, :] * (pos[:, None] + 1.0))
    kd = jnp.exp(lg[None, :] * (tc - 1.0 - pos[:, None]))
    rep = lambda t: jnp.repeat(t, RET_HEAD_DIM, axis=1)
    return dmask.astype(F32), rep(qd).astype(F32), rep(kd).astype(F32)


def _ssd_kernel(z_ref, xbc_ref, dtr_ref, convw_ref, convb_ref, dtb_ref, alog_ref, dexp_ref, ngain_ref,
                expand_ref, o_ref, xpad_ref, state_ref, xc_ref):
    @pl.when(pl.program_id(1) == 0)
    def _():
        xpad_ref[0:SUBLANES, :] = jnp.zeros((SUBLANES, CONV_CH), F32)
        state_ref[...] = jnp.zeros_like(state_ref)

    t = xbc_ref.shape[0]
    ts = SSD_CHUNK
    n = SSD_STATE
    gw = SSD_WIDTH // SSD_GROUPS

    xf = xbc_ref[...].astype(F32)
    xpad_ref[SUBLANES:, :] = xf
    conv = convb_ref[...] + convw_ref[CONV_WIDTH - 1:CONV_WIDTH, :] * xf
    for s in range(1, CONV_WIDTH):
        w_s = convw_ref[CONV_WIDTH - 1 - s:CONV_WIDTH - s, :]
        conv = conv + w_s * xpad_ref[SUBLANES - s:SUBLANES - s + t, :]
    xc_ref[...] = _silu(conv)
    xpad_ref[0:SUBLANES, :] = xf[t - SUBLANES:t, :]

    ri = lax.broadcasted_iota(jnp.int32, (ts, ts), 0)
    ci = lax.broadcasted_iota(jnp.int32, (ts, ts), 1)
    causal = ri >= ci
    tri = jnp.where(causal, 1.0, 0.0).astype(BF16)
    lane_lo = ci < SSD_HEAD_DIM
    expand = expand_ref[...]
    a_neg = -jnp.exp(alog_ref[...])

    for j in range(t // ts):
        rows = slice(j * ts, (j + 1) * ts)
        xs = xc_ref[rows, 0:SSD_WIDTH]
        bm = xc_ref[rows, SSD_WIDTH:SSD_WIDTH + SSD_GROUPS * n].astype(BF16)
        cm = xc_ref[rows, SSD_WIDTH + SSD_GROUPS * n:CONV_CH].astype(BF16)

        pre = dtr_ref[rows, :] + dtb_ref[...]
        dt = jnp.maximum(pre, 0.0) + jnp.log1p(jnp.exp(-jnp.abs(pre)))
        da = dt * a_neg
        acum = _dot_exact01(tri, da, left=True)
        acum_t = acum.T
        acum_e = _dot_exact01(expand, acum, left=False)
        dt_e = _dot_exact01(expand, dt, left=False)
        last_e = acum_e[ts - 1:ts, :]
        from_start = jnp.exp(acum_e)
        to_end = jnp.exp(last_e - acum_e)
        total = jnp.exp(last_e)

        xdt = xs * dt_e
        xdt_te = (xdt * to_end).astype(BF16)

        zf = z_ref[rows, :].astype(F32)
        gate = _silu(zf)
        for g in range(SSD_GROUPS):
            cg = cm[:, g * n:(g + 1) * n]
            bg = bm[:, g * n:(g + 1) * n]
            cb = lax.dot_general(cg, bg, (((1,), (1,)), ((), ())), preferred_element_type=F32)
            intra = []
            for pr in range(gw // LANES):
                lanes = slice(g * gw + pr * LANES, g * gw + (pr + 1) * LANES)
                xpair = xdt[:, lanes]
                y_pair = None
                for half in range(2):
                    hd = (g * gw + pr * LANES) // SSD_HEAD_DIM + half
                    seg = acum[:, hd:hd + 1] - acum_t[hd:hd + 1, :]
                    decay = jnp.exp(jnp.where(causal, seg, -1e30))
                    wmat = (cb * decay).astype(BF16)
                    keep = lane_lo if half == 0 else jnp.logical_not(lane_lo)
                    xh = jnp.where(keep, xpair, 0.0).astype(BF16)
                    term = _dot(wmat, xh)
                    y_pair = term if y_pair is None else y_pair + term
                intra.append(y_pair)
            y_intra = jnp.concatenate(intra, axis=1)
            gl = slice(g * gw, (g + 1) * gw)
            st = state_ref[:, gl]
            y_inter = _dot(cg, st.astype(BF16)) * from_start[:, gl]
            upd = lax.dot_general(bg, xdt_te[:, gl], (((0,), (0,)), ((), ())), preferred_element_type=F32)
            state_ref[:, gl] = st * total[:, gl] + upd
            y = (y_intra + y_inter + xs[:, gl] * dexp_ref[:, gl]) * gate[:, gl]
            ms = jnp.mean(y * y, axis=-1, keepdims=True)
            o_ref[rows, gl] = (y * lax.rsqrt(ms + LN_EPS) * ngain_ref[:, gl]).astype(o_ref.dtype)


def _ssd(rest, dtr, convw, convb, dtb, a_log, dexp, ngain, expand, batch, seq):
    t = SSD_STEP
    nc = seq // t
    row = lambda b, c: b * nc + c
    const2 = lambda b, c: (0, 0)
    return pl.pallas_call(
        _ssd_kernel,
        grid=(batch, nc),
        in_specs=[
            pl.BlockSpec((t, SSD_WIDTH), lambda b, c: (row(b, c), 2)),
            pl.BlockSpec((t, CONV_CH), lambda b, c: (row(b, c), 2)),
            pl.BlockSpec((t, LANES), lambda b, c: (row(b, c), 0)),
            pl.BlockSpec((CONV_WIDTH, CONV_CH), const2),
            pl.BlockSpec((1, CONV_CH), const2),
            pl.BlockSpec((1, LANES), const2),
            pl.BlockSpec((1, LANES), const2),
            pl.BlockSpec((1, SSD_WIDTH), const2),
            pl.BlockSpec((1, SSD_WIDTH), const2),
            pl.BlockSpec((LANES, SSD_WIDTH), const2),
        ],
        out_specs=pl.BlockSpec((t, SSD_WIDTH), lambda b, c: (row(b, c), 0)),
        out_shape=jax.ShapeDtypeStruct((batch * seq, SSD_WIDTH), BF16),
        scratch_shapes=[
            pltpu.VMEM((t + SUBLANES, CONV_CH), F32),
            pltpu.VMEM((SSD_STATE, SSD_WIDTH), F32),
            pltpu.VMEM((t, CONV_CH), F32),
        ],
        compiler_params=_cparams(("parallel", "arbitrary")),
        name="ssd",
    )(rest, rest, dtr, convw, convb, dtb, a_log, dexp, ngain, expand)


def _outproj_ln_kernel(ret_ref, ssd_ref, x_ref, wo_ref, gain_ref, bias_ref, o_ref, *, alpha):
    for r in range(OUT_TM // OUT_ROWS):
        rows = slice(r * OUT_ROWS, (r + 1) * OUT_ROWS)
        mix = _dot(ret_ref[rows, :], wo_ref[0:RET_WIDTH, :]) + _dot(ssd_ref[rows, :], wo_ref[RET_WIDTH:, :])
        y = alpha * x_ref[rows, :] + mix
        o_ref[rows, :] = _layer_norm_rows(y, gain_ref[...], bias_ref[...])


def _outproj_ln(ret, ssd, x, wo, gain, bias, alpha):
    m, d = x.shape
    return pl.pallas_call(
        functools.partial(_outproj_ln_kernel, alpha=alpha),
        grid=(m // OUT_TM,),
        in_specs=[
            pl.BlockSpec((OUT_TM, RET_WIDTH), lambda i: (i, 0)),
            pl.BlockSpec((OUT_TM, SSD_WIDTH), lambda i: (i, 0)),
            pl.BlockSpec((OUT_TM, d), lambda i: (i, 0)),
            pl.BlockSpec((RET_WIDTH + SSD_WIDTH, d), lambda i: (0, 0), pipeline_mode=pl.Buffered(1)),
            pl.BlockSpec((1, d), lambda i: (0, 0)),
            pl.BlockSpec((1, d), lambda i: (0, 0)),
        ],
        out_specs=pl.BlockSpec((OUT_TM, d), lambda i: (i, 0)),
        out_shape=jax.ShapeDtypeStruct((m, d), F32),
        compiler_params=_cparams(("parallel",)),
        name="outproj_ln",
    )(ret, ssd, x, wo, gain, bias)


def _even_odd_head_columns(w):
    d, n = w.shape
    w4 = w.reshape(d, n // RET_HEAD_DIM, RET_HEAD_DIM // 2, 2)
    return jnp.swapaxes(w4, 2, 3).reshape(d, n)


def _col_tiles(w, tn):
    k, n = w.shape
    return jnp.swapaxes(w.reshape(k, n // tn, tn), 0, 1).astype(BF16)


def kernel(x, positions, ffn1_w_gate, ffn1_w_up, ffn1_w_down, ln1_gain, ln1_bias, mix_w_in, ret_gn_gain, ret_gn_bias, ssd_conv_w, ssd_conv_b, ssd_dt_bias, ssd_a_log, ssd_d, ssd_norm_gain, mix_w_out, ln2_gain, ln2_bias, ffn2_w_gate, ffn2_w_up, ffn2_w_down, ln3_gain, ln3_bias):
    batch, seq, d = x.shape
    depth = ffn1_w_gate.shape[0]
    alpha = (2.0 * depth) ** 0.25
    m = batch * seq
    h = x.reshape(m, d)

    half = RET_HEAD_DIM // 2
    inv_freq = (1.0 / (ROPE_BASE ** jnp.linspace(0.0, 1.0, half, dtype=F32))).reshape(1, half)
    cos, sin = _rope_tables(positions.reshape(m, 1), inv_freq)
    dmask, qd, kd = _retention_consts(RET_CHUNK)
    expand = (jnp.arange(SSD_WIDTH)[None, :] // SSD_HEAD_DIM == jnp.arange(LANES)[:, None]).astype(BF16)
    row = lambda v: v.reshape(1, -1).astype(F32)
    pad_lanes = lambda v: jnp.pad(v.astype(F32), (0, LANES - v.shape[0])).reshape(1, LANES)

    for layer in range(depth):
        w_in = mix_w_in[layer]
        rw = RET_WIDTH
        w_qk = _col_tiles(_even_odd_head_columns(w_in[:, 0:2 * rw]), PROJ_TN_QK)
        n_rest = 2 * rw + SSD_WIDTH + CONV_CH
        w_rest = _col_tiles(w_in[:, 2 * rw:2 * rw + n_rest], PROJ_TN_REST)
        w_dt = jnp.pad(w_in[:, 2 * rw + n_rest:], ((0, 0), (0, LANES - SSD_HEADS))).astype(BF16)

        h = _ffn_ln(h, _col_tiles(ffn1_w_gate[layer], FFN_TF), _col_tiles(ffn1_w_up[layer], FFN_TF),
                    ffn1_w_down[layer].astype(BF16), row(ln1_gain[layer]), row(ln1_bias[layer]), alpha)

        qk, rest, dtr = _in_proj(h, w_qk, w_rest, w_dt, cos, sin)

        ret = _retention(qk, rest, dmask, qd, kd, row(ret_gn_gain[layer]), row(ret_gn_bias[layer]),
                         batch, seq)
        dexp = jnp.repeat(ssd_d[layer].astype(F32), SSD_HEAD_DIM).reshape(1, SSD_WIDTH)
        ssd = _ssd(rest, dtr, ssd_conv_w[layer].astype(F32), row(ssd_conv_b[layer]),
                   pad_lanes(ssd_dt_bias[layer]), pad_lanes(ssd_a_log[layer]), dexp, row(ssd_norm_gain[layer]), expand,
                   batch, seq)

        h = _outproj_ln(ret, ssd, h, mix_w_out[layer].astype(BF16), row(ln2_gain[layer]),
                        row(ln2_bias[layer]), alpha)

        h = _ffn_ln(h, _col_tiles(ffn2_w_gate[layer], FFN_TF), _col_tiles(ffn2_w_up[layer], FFN_TF),
                    ffn2_w_down[layer].astype(BF16), row(ln3_gain[layer]), row(ln3_bias[layer]), alpha)

    return h.reshape(batch, seq, d)
```

```python
import functools
import math

import jax
import jax.numpy as jnp
import numpy as np
from jax import lax
from jax.experimental import pallas as pl
from jax.experimental.pallas import tpu as pltpu

F32 = jnp.float32
BF16 = jnp.bfloat16

D_MODEL = 2048
RET_WIDTH = 1024
RET_HEAD_DIM = 256
RET_HEADS = 4
SSD_WIDTH = 1024
SSD_HEAD_DIM = 64
SSD_HEADS = 16
SSD_GROUPS = 2
SSD_STATE = 128
CONV_WIDTH = 4
CONV_CH = SSD_WIDTH + 2 * SSD_GROUPS * SSD_STATE
D_FF = 5632
ROPE_BASE = 10000.0
LN_EPS = 1e-5
FFN_RES_WEIGHT = 0.5

LANES = 128
SUBLANES = 8
VMEM_BYTES_V7X = 64 * 1024 * 1024
VMEM_LIMIT = VMEM_BYTES_V7X - 4 * 1024 * 1024
VMEM_LIMIT_FFN = VMEM_BYTES_V7X - 1024 * 1024

FFN_TM = 1024
FFN_TF = 512
FFN_ROWS = 512
FFN_LN_ROWS = 256
PROJ_TM = 1024
PROJ_TN_QK = 512
PROJ_TN_REST = 768
OUT_TM = 1024
OUT_ROWS = 128
RET_CHUNK = 256
SSD_CHUNK = 128
SSD_STEP = 256
ROPE_TM = 1024
W_PREP_ROWS = 256


def _cparams(sem, vmem_limit=VMEM_LIMIT):
    return pltpu.CompilerParams(dimension_semantics=sem, vmem_limit_bytes=vmem_limit)


def _silu(v):
    return v / (1.0 + jnp.exp(-v))


def _layer_norm_rows(y, gain, bias):
    mu = jnp.mean(y, axis=-1, keepdims=True)
    d = y - mu
    var = jnp.mean(d * d, axis=-1, keepdims=True)
    return d * lax.rsqrt(var + LN_EPS) * gain + bias


def _split3(v):
    h0 = v.astype(BF16)
    r1 = v - h0.astype(F32)
    h1 = r1.astype(BF16)
    h2 = (r1 - h1.astype(F32)).astype(BF16)
    return h0, h1, h2


def _dot(a, b):
    return jnp.dot(a, b, preferred_element_type=F32)


def _dot_exact01(m01, v, left=True):
    parts = _split3(v)
    if left:
        return _dot(m01, parts[0]) + _dot(m01, parts[1]) + _dot(m01, parts[2])
    return _dot(parts[0], m01) + _dot(parts[1], m01) + _dot(parts[2], m01)


def _ffn_ln_kernel(x_ref, wg_ref, wu_ref, wd_ref, gain_ref, bias_ref, o_ref, xb_ref, *, alpha, n_f):
    f = pl.program_id(1)

    @pl.when(f == 0)
    def _():
        xb_ref[...] = x_ref[...].astype(BF16)
        o_ref[...] = jnp.zeros_like(o_ref)

    def partial_out(rows):
        xb = xb_ref[rows, :]
        h = (_silu(_dot(xb, wg_ref[...])) * _dot(xb, wu_ref[...])).astype(BF16)
        return _dot(h, wd_ref[...])

    @pl.when(f < n_f - 1)
    def _():
        for r in range(FFN_TM // FFN_ROWS):
            rows = slice(r * FFN_ROWS, (r + 1) * FFN_ROWS)
            o_ref[rows, :] += partial_out(rows)

    @pl.when(f == n_f - 1)
    def _():
        for r in range(FFN_TM // FFN_LN_ROWS):
            rows = slice(r * FFN_LN_ROWS, (r + 1) * FFN_LN_ROWS)
            y = alpha * x_ref[rows, :] + FFN_RES_WEIGHT * (o_ref[rows, :] + partial_out(rows))
            o_ref[rows, :] = _layer_norm_rows(y, gain_ref[...], bias_ref[...])


def _ffn_ln(x, wg, wu, wd, gain, bias, alpha):
    m, d = x.shape
    n_f = D_FF // FFN_TF
    return pl.pallas_call(
        functools.partial(_ffn_ln_kernel, alpha=alpha, n_f=n_f),
        grid=(m // FFN_TM, n_f),
        in_specs=[
            pl.BlockSpec((FFN_TM, d), lambda i, f: (i, 0)),
            pl.BlockSpec((d, FFN_TF), lambda i, f: (0, f)),
            pl.BlockSpec((d, FFN_TF), lambda i, f: (0, f)),
            pl.BlockSpec((FFN_TF, d), lambda i, f: (f, 0)),
            pl.BlockSpec((1, d), lambda i, f: (0, 0)),
            pl.BlockSpec((1, d), lambda i, f: (0, 0)),
        ],
        out_specs=pl.BlockSpec((FFN_TM, d), lambda i, f: (i, 0)),
        out_shape=jax.ShapeDtypeStruct((m, d), F32),
        scratch_shapes=[pltpu.VMEM((FFN_TM, d), BF16)],
        compiler_params=_cparams(("parallel", "arbitrary"), VMEM_LIMIT_FFN),
        name="ffn_ln",
    )(x, wg, wu, wd, gain, bias)


def _rope_tab_kernel(pos_ref, invf_ref, cos_ref, sin_ref):
    theta = pos_ref[...].astype(F32) * invf_ref[...]
    cos_ref[...] = jnp.cos(theta)
    sin_ref[...] = jnp.sin(theta)


def _rope_tables(pos_col, inv_freq):
    m = pos_col.shape[0]
    half = inv_freq.shape[1]
    return pl.pallas_call(
        _rope_tab_kernel,
        grid=(m // ROPE_TM,),
        in_specs=[
            pl.BlockSpec((ROPE_TM, 1), lambda i: (i, 0)),
            pl.BlockSpec((1, half), lambda i: (0, 0)),
        ],
        out_specs=[
            pl.BlockSpec((ROPE_TM, half), lambda i: (i, 0)),
            pl.BlockSpec((ROPE_TM, half), lambda i: (i, 0)),
        ],
        out_shape=[jax.ShapeDtypeStruct((m, half), F32)] * 2,
        compiler_params=_cparams(("parallel",)),
        name="rope_tab",
    )(pos_col, inv_freq)


def _in_proj_kernel(x_ref, wqk_ref, wrest_ref, wdt_ref, cos_ref, sin_ref, qk_ref, rest_ref, dt_ref, xb_ref,
                    *, n_qk, k_scale):
    j = pl.program_id(1)

    @pl.when(j == 0)
    def _():
        xb_ref[...] = x_ref[...].astype(BF16)
        dt_ref[...] = _dot(xb_ref[...], wdt_ref[...])

    @pl.when(j < n_qk)
    def _():
        acc = _dot(xb_ref[...], wqk_ref[...])
        scale = jnp.where(j >= n_qk // 2, k_scale, 1.0).astype(F32)
        c = cos_ref[...] * scale
        s = sin_ref[...] * scale
        half = RET_HEAD_DIM // 2
        for hd in range(acc.shape[1] // RET_HEAD_DIM):
            lo = hd * RET_HEAD_DIM
            e = acc[:, lo:lo + half]
            o = acc[:, lo + half:lo + RET_HEAD_DIM]
            qk_ref[:, lo:lo + half] = (e * c - o * s).astype(qk_ref.dtype)
            qk_ref[:, lo + half:lo + RET_HEAD_DIM] = (o * c + e * s).astype(qk_ref.dtype)

    @pl.when(j >= n_qk)
    def _():
        rest_ref[...] = _dot(xb_ref[...], wrest_ref[...]).astype(rest_ref.dtype)


def _in_proj(x, w_qk, w_rest, w_dt, cos, sin):
    m, k = x.shape
    half = RET_HEAD_DIM // 2
    tn_qk, tn_rest = PROJ_TN_QK, PROJ_TN_REST
    n_qk = w_qk.shape[1] // tn_qk
    n_rest = w_rest.shape[1] // tn_rest
    qk_j = lambda j: jnp.minimum(j, n_qk - 1)
    rest_j = lambda j: jnp.maximum(j - n_qk, 0)
    return pl.pallas_call(
        functools.partial(_in_proj_kernel, n_qk=n_qk, k_scale=RET_HEAD_DIM ** -0.5),
        grid=(m // PROJ_TM, n_qk + n_rest),
        in_specs=[
            pl.BlockSpec((PROJ_TM, k), lambda i, j: (i, 0)),
            pl.BlockSpec((k, tn_qk), lambda i, j: (0, qk_j(j))),
            pl.BlockSpec((k, tn_rest), lambda i, j: (0, rest_j(j))),
            pl.BlockSpec((k, LANES), lambda i, j: (0, 0)),
            pl.BlockSpec((PROJ_TM, half), lambda i, j: (i, 0)),
            pl.BlockSpec((PROJ_TM, half), lambda i, j: (i, 0)),
        ],
        out_specs=[
            pl.BlockSpec((PROJ_TM, tn_qk), lambda i, j: (i, qk_j(j))),
            pl.BlockSpec((PROJ_TM, tn_rest), lambda i, j: (i, rest_j(j))),
            pl.BlockSpec((PROJ_TM, LANES), lambda i, j: (i, 0)),
        ],
        out_shape=[
            jax.ShapeDtypeStruct((m, n_qk * tn_qk), BF16),
            jax.ShapeDtypeStruct((m, n_rest * tn_rest), BF16),
            jax.ShapeDtypeStruct((m, LANES), F32),
        ],
        scratch_shapes=[pltpu.VMEM((PROJ_TM, k), BF16)],
        compiler_params=_cparams(("parallel", "arbitrary")),
        name="in_proj",
    )(x, w_qk, w_rest, w_dt, cos, sin)


def _ret_log_gamma(h):
    return math.log(1.0 - 2.0 ** (-5.0 - h))


def _retention_kernel(q_ref, k_ref, v_ref, g_ref, dmask_ref, qd_ref, kd_ref, gain_ref, bias_ref,
                      o_ref, state_ref):
    @pl.when(pl.program_id(1) == 0)
    def _():
        state_ref[...] = jnp.zeros_like(state_ref)

    tc = q_ref.shape[0]
    dh = RET_HEAD_DIM
    for h in range(RET_HEADS):
        sl = slice(h * dh, (h + 1) * dh)
        q = q_ref[:, sl]
        k = k_ref[:, sl]
        v = v_ref[:, sl]
        scores = lax.dot_general(q, k, (((1,), (1,)), ((), ())), preferred_element_type=F32)
        p = (scores * dmask_ref[h]).astype(BF16)
        state = state_ref[h]
        o = _dot(p, v) + _dot(q, state.astype(BF16)) * qd_ref[:, sl]
        vkd = (v.astype(F32) * kd_ref[:, sl]).astype(BF16)
        kv = lax.dot_general(k, vkd, (((0,), (0,)), ((), ())), preferred_element_type=F32)
        state_ref[h] = state * math.exp(_ret_log_gamma(h) * tc) + kv
        mu = jnp.mean(o, axis=-1, keepdims=True)
        d = o - mu
        var = jnp.mean(d * d, axis=-1, keepdims=True)
        on = d * lax.rsqrt(var + LN_EPS) * gain_ref[:, sl] + bias_ref[:, sl]
        o_ref[:, sl] = (_silu(g_ref[:, sl].astype(F32)) * on).astype(o_ref.dtype)


def _retention(qk, rest, dmask, qd, kd, gain, bias, batch, seq):
    tc = RET_CHUNK
    nc = seq // tc
    w = RET_WIDTH
    row = lambda b, c: b * nc + c
    const2 = lambda b, c: (0, 0)
    return pl.pallas_call(
        _retention_kernel,
        grid=(batch, nc),
        in_specs=[
            pl.BlockSpec((tc, w), lambda b, c: (row(b, c), 0)),
            pl.BlockSpec((tc, w), lambda b, c: (row(b, c), 1)),
            pl.BlockSpec((tc, w), lambda b, c: (row(b, c), 0)),
            pl.BlockSpec((tc, w), lambda b, c: (row(b, c), 1)),
            pl.BlockSpec((RET_HEADS, tc, tc), lambda b, c: (0, 0, 0)),
            pl.BlockSpec((tc, w), const2),
            pl.BlockSpec((tc, w), const2),
            pl.BlockSpec((1, w), const2),
            pl.BlockSpec((1, w), const2),
        ],
        out_specs=pl.BlockSpec((tc, w), lambda b, c: (row(b, c), 0)),
        out_shape=jax.ShapeDtypeStruct((batch * seq, w), BF16),
        scratch_shapes=[pltpu.VMEM((RET_HEADS, RET_HEAD_DIM, RET_HEAD_DIM), F32)],
        compiler_params=_cparams(("parallel", "arbitrary")),
        name="retention",
    )(qk, qk, rest, rest, dmask, qd, kd, gain, bias)


def _retention_consts(tc):
    lg = jnp.asarray([_ret_log_gamma(h) for h in range(RET_HEADS)], F32)
    pos = jnp.arange(tc, dtype=F32)
    rel = pos[:, None] - pos[None, :]
    dmask = jnp.where(rel >= 0, jnp.exp(lg[:, None, None] * jnp.maximum(rel, 0.0)), 0.0)
    qd = jnp.exp(lg[None, :] * (pos[:, None] + 1.0))
    kd = jnp.exp(lg[None, :] * (tc - 1.0 - pos[:, None]))
    rep = lambda t: jnp.repeat(t, RET_HEAD_DIM, axis=1)
    return dmask.astype(F32), rep(qd).astype(F32), rep(kd).astype(F32)


def _ssd_kernel(z_ref, xbc_ref, dtr_ref, convw_ref, convb_ref, dtb_ref, alog_ref, dexp_ref, ngain_ref,
                expand_ref, o_ref, xpad_ref, state_ref, xc_ref):
    @pl.when(pl.program_id(1) == 0)
    def _():
        xpad_ref[0:SUBLANES, :] = jnp.zeros((SUBLANES, CONV_CH), F32)
        state_ref[...] = jnp.zeros_like(state_ref)

    t = xbc_ref.shape[0]
    ts = SSD_CHUNK
    n = SSD_STATE
    gw = SSD_WIDTH // SSD_GROUPS

    xf = xbc_ref[...].astype(F32)
    xpad_ref[SUBLANES:, :] = xf
    conv = convb_ref[...] + convw_ref[CONV_WIDTH - 1:CONV_WIDTH, :] * xf
    for s in range(1, CONV_WIDTH):
        w_s = convw_ref[CONV_WIDTH - 1 - s:CONV_WIDTH - s, :]
        conv = conv + w_s * xpad_ref[SUBLANES - s:SUBLANES - s + t, :]
    xc_ref[...] = _silu(conv)
    xpad_ref[0:SUBLANES, :] = xf[t - SUBLANES:t, :]

    ri = lax.broadcasted_iota(jnp.int32, (ts, ts), 0)
    ci = lax.broadcasted_iota(jnp.int32, (ts, ts), 1)
    causal = ri >= ci
    tri = jnp.where(causal, 1.0, 0.0).astype(BF16)
    lane_lo = ci < SSD_HEAD_DIM
    expand = expand_ref[...]
    a_neg = -jnp.exp(alog_ref[...])

    for j in range(t // ts):
        rows = slice(j * ts, (j + 1) * ts)
        xs = xc_ref[rows, 0:SSD_WIDTH]
        bm = xc_ref[rows, SSD_WIDTH:SSD_WIDTH + SSD_GROUPS * n].astype(BF16)
        cm = xc_ref[rows, SSD_WIDTH + SSD_GROUPS * n:CONV_CH].astype(BF16)

        pre = dtr_ref[rows, :] + dtb_ref[...]
        dt = jnp.maximum(pre, 0.0) + jnp.log1p(jnp.exp(-jnp.abs(pre)))
        da = dt * a_neg
        acum = _dot_exact01(tri, da, left=True)
        acum_t = acum.T
        acum_e = _dot_exact01(expand, acum, left=False)
        dt_e = _dot_exact01(expand, dt, left=False)
        last_e = acum_e[ts - 1:ts, :]
        from_start = jnp.exp(acum_e)
        to_end = jnp.exp(last_e - acum_e)
        total = jnp.exp(last_e)

        xdt = xs * dt_e
        xdt_te = (xdt * to_end).astype(BF16)

        zf = z_ref[rows, :].astype(F32)
        gate = _silu(zf)
        for g in range(SSD_GROUPS):
            cg = cm[:, g * n:(g + 1) * n]
            bg = bm[:, g * n:(g + 1) * n]
            cb = lax.dot_general(cg, bg, (((1,), (1,)), ((), ())), preferred_element_type=F32)
            intra = []
            for pr in range(gw // LANES):
                lanes = slice(g * gw + pr * LANES, g * gw + (pr + 1) * LANES)
                xpair = xdt[:, lanes]
                y_pair = None
                for half in range(2):
                    hd = (g * gw + pr * LANES) // SSD_HEAD_DIM + half
                    seg = acum[:, hd:hd + 1] - acum_t[hd:hd + 1, :]
                    decay = jnp.exp(jnp.where(causal, seg, -1e30))
                    wmat = (cb * decay).astype(BF16)
                    keep = lane_lo if half == 0 else jnp.logical_not(lane_lo)
                    xh = jnp.where(keep, xpair, 0.0).astype(BF16)
                    term = _dot(wmat, xh)
                    y_pair = term if y_pair is None else y_pair + term
                intra.append(y_pair)
            y_intra = jnp.concatenate(intra, axis=1)
            gl = slice(g * gw, (g + 1) * gw)
            st = state_ref[:, gl]
            y_inter = _dot(cg, st.astype(BF16)) * from_start[:, gl]
            upd = lax.dot_general(bg, xdt_te[:, gl], (((0,), (0,)), ((), ())), preferred_element_type=F32)
            state_ref[:, gl] = st * total[:, gl] + upd
            y = (y_intra + y_inter + xs[:, gl] * dexp_ref[:, gl]) * gate[:, gl]
            ms = jnp.mean(y * y, axis=-1, keepdims=True)
            o_ref[rows, gl] = (y * lax.rsqrt(ms + LN_EPS) * ngain_ref[:, gl]).astype(o_ref.dtype)


def _ssd(rest, dtr, convw, convb, dtb, a_log, dexp, ngain, expand, batch, seq):
    t = SSD_STEP
    nc = seq // t
    row = lambda b, c: b * nc + c
    const2 = lambda b, c: (0, 0)
    return pl.pallas_call(
        _ssd_kernel,
        grid=(batch, nc),
        in_specs=[
            pl.BlockSpec((t, SSD_WIDTH), lambda b, c: (row(b, c), 2)),
            pl.BlockSpec((t, CONV_CH), lambda b, c: (row(b, c), 2)),
            pl.BlockSpec((t, LANES), lambda b, c: (row(b, c), 0)),
            pl.BlockSpec((CONV_WIDTH, CONV_CH), const2),
            pl.BlockSpec((1, CONV_CH), const2),
            pl.BlockSpec((1, LANES), const2),
            pl.BlockSpec((1, LANES), const2),
            pl.BlockSpec((1, SSD_WIDTH), const2),
            pl.BlockSpec((1, SSD_WIDTH), const2),
            pl.BlockSpec((LANES, SSD_WIDTH), const2),
        ],
        out_specs=pl.BlockSpec((t, SSD_WIDTH), lambda b, c: (row(b, c), 0)),
        out_shape=jax.ShapeDtypeStruct((batch * seq, SSD_WIDTH), BF16),
        scratch_shapes=[
            pltpu.VMEM((t + SUBLANES, CONV_CH), F32),
            pltpu.VMEM((SSD_STATE, SSD_WIDTH), F32),
            pltpu.VMEM((t, CONV_CH), F32),
        ],
        compiler_params=_cparams(("parallel", "arbitrary")),
        name="ssd",
    )(rest, rest, dtr, convw, convb, dtb, a_log, dexp, ngain, expand)


def _outproj_ln_kernel(ret_ref, ssd_ref, x_ref, wo_ref, gain_ref, bias_ref, o_ref, *, alpha):
    for r in range(OUT_TM // OUT_ROWS):
        rows = slice(r * OUT_ROWS, (r + 1) * OUT_ROWS)
        mix = _dot(ret_ref[rows, :], wo_ref[0:RET_WIDTH, :]) + _dot(ssd_ref[rows, :], wo_ref[RET_WIDTH:, :])
        y = alpha * x_ref[rows, :] + mix
        o_ref[rows, :] = _layer_norm_rows(y, gain_ref[...], bias_ref[...])


def _outproj_ln(ret, ssd, x, wo, gain, bias, alpha):
    m, d = x.shape
    return pl.pallas_call(
        functools.partial(_outproj_ln_kernel, alpha=alpha),
        grid=(m // OUT_TM,),
        in_specs=[
            pl.BlockSpec((OUT_TM, RET_WIDTH), lambda i: (i, 0)),
            pl.BlockSpec((OUT_TM, SSD_WIDTH), lambda i: (i, 0)),
            pl.BlockSpec((OUT_TM, d), lambda i: (i, 0)),
            pl.BlockSpec((RET_WIDTH + SSD_WIDTH, d), lambda i: (0, 0), pipeline_mode=pl.Buffered(1)),
            pl.BlockSpec((1, d), lambda i: (0, 0)),
            pl.BlockSpec((1, d), lambda i: (0, 0)),
        ],
        out_specs=pl.BlockSpec((OUT_TM, d), lambda i: (i, 0)),
        out_shape=jax.ShapeDtypeStruct((m, d), F32),
        compiler_params=_cparams(("parallel",)),
        name="outproj_ln",
    )(ret, ssd, x, wo, gain, bias)


def _w_in_prep_kernel(w_ref, perm_ref, qk_ref, rest_ref, dt_ref):
    n_qk = qk_ref.shape[1]
    n_rest = rest_ref.shape[1]
    for hd in range(n_qk // RET_HEAD_DIM):
        cols = slice(hd * RET_HEAD_DIM, (hd + 1) * RET_HEAD_DIM)
        qk_ref[:, cols] = _dot(w_ref[:, cols].astype(BF16), perm_ref[...]).astype(BF16)
    rest_ref[...] = w_ref[:, n_qk:n_qk + n_rest].astype(BF16)
    dt_ref[...] = jnp.zeros(dt_ref.shape, dt_ref.dtype)
    dt_ref[:, 0:SSD_HEADS] = w_ref[:, n_qk + n_rest:n_qk + n_rest + SSD_HEADS].astype(BF16)


def _w_in_prep(w_in, perm):
    k, n = w_in.shape
    n_qk = 2 * RET_WIDTH
    n_rest = 2 * RET_WIDTH + SSD_WIDTH + CONV_CH
    return pl.pallas_call(
        _w_in_prep_kernel,
        grid=(k // W_PREP_ROWS,),
        in_specs=[
            pl.BlockSpec((W_PREP_ROWS, n), lambda i: (i, 0)),
            pl.BlockSpec((RET_HEAD_DIM, RET_HEAD_DIM), lambda i: (0, 0)),
        ],
        out_specs=[
            pl.BlockSpec((W_PREP_ROWS, n_qk), lambda i: (i, 0)),
            pl.BlockSpec((W_PREP_ROWS, n_rest), lambda i: (i, 0)),
            pl.BlockSpec((W_PREP_ROWS, LANES), lambda i: (i, 0)),
        ],
        out_shape=[
            jax.ShapeDtypeStruct((k, n_qk), BF16),
            jax.ShapeDtypeStruct((k, n_rest), BF16),
            jax.ShapeDtypeStruct((k, LANES), BF16),
        ],
        compiler_params=_cparams(("parallel",)),
        name="w_in_prep",
    )(w_in, perm)


def _even_odd_permutation():
    half = RET_HEAD_DIM // 2
    j = jnp.arange(RET_HEAD_DIM)
    src = jnp.where(j < half, 2 * j, 2 * (j - half) + 1)
    return (jnp.arange(RET_HEAD_DIM)[:, None] == src[None, :]).astype(BF16)


def kernel(x, positions, ffn1_w_gate, ffn1_w_up, ffn1_w_down, ln1_gain, ln1_bias, mix_w_in, ret_gn_gain, ret_gn_bias, ssd_conv_w, ssd_conv_b, ssd_dt_bias, ssd_a_log, ssd_d, ssd_norm_gain, mix_w_out, ln2_gain, ln2_bias, ffn2_w_gate, ffn2_w_up, ffn2_w_down, ln3_gain, ln3_bias):
    batch, seq, d = x.shape
    depth = ffn1_w_gate.shape[0]
    alpha = (2.0 * depth) ** 0.25
    m = batch * seq
    h = x.reshape(m, d)

    half = RET_HEAD_DIM // 2
    inv_freq = (1.0 / (ROPE_BASE ** jnp.linspace(0.0, 1.0, half, dtype=F32))).reshape(1, half)
    cos, sin = _rope_tables(positions.reshape(m, 1), inv_freq)
    dmask, qd, kd = _retention_consts(RET_CHUNK)
    perm = _even_odd_permutation()
    expand = (jnp.arange(SSD_WIDTH)[None, :] // SSD_HEAD_DIM == jnp.arange(LANES)[:, None]).astype(BF16)
    row = lambda v: v.reshape(1, -1).astype(F32)
    pad_lanes = lambda v: jnp.pad(v.astype(F32), (0, LANES - v.shape[0])).reshape(1, LANES)

    for layer in range(depth):
        w_qk, w_rest, w_dt = _w_in_prep(mix_w_in[layer], perm)

        h = _ffn_ln(h, ffn1_w_gate[layer].astype(BF16), ffn1_w_up[layer].astype(BF16),
                    ffn1_w_down[layer].astype(BF16), row(ln1_gain[layer]), row(ln1_bias[layer]), alpha)

        qk, rest, dtr = _in_proj(h, w_qk, w_rest, w_dt, cos, sin)

        ret = _retention(qk, rest, dmask, qd, kd, row(ret_gn_gain[layer]), row(ret_gn_bias[layer]),
                         batch, seq)
        dexp = jnp.repeat(ssd_d[layer].astype(F32), SSD_HEAD_DIM).reshape(1, SSD_WIDTH)
        ssd = _ssd(rest, dtr, ssd_conv_w[layer].astype(F32), row(ssd_conv_b[layer]),
                   pad_lanes(ssd_dt_bias[layer]), pad_lanes(ssd_a_log[layer]), dexp, row(ssd_norm_gain[layer]), expand,
                   batch, seq)

        h = _outproj_ln(ret, ssd, h, mix_w_out[layer].astype(BF16), row(ln2_gain[layer]),
                        row(ln2_bias[layer]), alpha)

        h = _ffn_ln(h, ffn2_w_gate[layer].astype(BF16), ffn2_w_up[layer].astype(BF16),
                    ffn2_w_down[layer].astype(BF16), row(ln3_gain[layer]), row(ln3_bias[layer]), alpha)

    return h.reshape(batch, seq, d)
```

```python
import functools
import math

import jax
import jax.numpy as jnp
from jax import lax
from jax.experimental import pallas as pl
from jax.experimental.pallas import tpu as pltpu

F32 = jnp.float32
BF16 = jnp.bfloat16

D_MODEL = 2048
RET_WIDTH = 1024
RET_HEAD_DIM = 256
RET_HEADS = 4
SSD_WIDTH = 1024
SSD_HEAD_DIM = 64
SSD_HEADS = 16
SSD_GROUPS = 2
SSD_STATE = 128
CONV_WIDTH = 4
CONV_CH = SSD_WIDTH + 2 * SSD_GROUPS * SSD_STATE
D_FF = 5632
ROPE_BASE = 10000.0
LN_EPS = 1e-5
FFN_RES_WEIGHT = 0.5

LANES = 128
SUBLANES = 8
VMEM_BYTES_V7X = 64 * 1024 * 1024
VMEM_LIMIT = VMEM_BYTES_V7X - 4 * 1024 * 1024
VMEM_LIMIT_FFN = VMEM_BYTES_V7X - 1024 * 1024

FFN_TM = 1024
FFN_TF = 512
FFN_ROWS = 512
FFN_LN_ROWS = 256
PROJ_TM = 1024
PROJ_TN_QK = 512
PROJ_TN_REST = 768
OUT_TM = 1024
OUT_ROWS = 128
RET_CHUNK = 256
SSD_CHUNK = 128
MIX_STEP = 256
ROPE_TM = 1024
W_PREP_ROWS = 256


def _cparams(sem, vmem_limit=VMEM_LIMIT):
    return pltpu.CompilerParams(dimension_semantics=sem, vmem_limit_bytes=vmem_limit)


def _silu(v):
    return v / (1.0 + jnp.exp(-v))


def _layer_norm_rows(y, gain, bias):
    mu = jnp.mean(y, axis=-1, keepdims=True)
    d = y - mu
    var = jnp.mean(d * d, axis=-1, keepdims=True)
    return d * lax.rsqrt(var + LN_EPS) * gain + bias


def _split3(v):
    h0 = v.astype(BF16)
    r1 = v - h0.astype(F32)
    h1 = r1.astype(BF16)
    h2 = (r1 - h1.astype(F32)).astype(BF16)
    return h0, h1, h2


def _dot(a, b):
    return jnp.dot(a, b, preferred_element_type=F32)


def _dot_exact01(m01, v, left=True):
    parts = _split3(v)
    if left:
        return _dot(m01, parts[0]) + _dot(m01, parts[1]) + _dot(m01, parts[2])
    return _dot(parts[0], m01) + _dot(parts[1], m01) + _dot(parts[2], m01)


def _ffn_ln_kernel(*refs, alpha, n_f, n_cast):
    x_ref, wg_ref, wu_ref, wd_ref, gain_ref, bias_ref = refs[:6]
    cast_in = refs[6:6 + n_cast]
    o_ref = refs[6 + n_cast]
    cast_out = refs[7 + n_cast:7 + 2 * n_cast]
    xb_ref = refs[7 + 2 * n_cast]
    f = pl.program_id(1)

    for src, dst in zip(cast_in, cast_out):
        dst[...] = src[...].astype(dst.dtype)

    def partial_out(rows):
        xb = xb_ref[rows, :]
        h = (_silu(_dot(xb, wg_ref[...])) * _dot(xb, wu_ref[...])).astype(BF16)
        return _dot(h, wd_ref[...])

    @pl.when(f == 0)
    def _():
        for r in range(FFN_TM // FFN_ROWS):
            rows = slice(r * FFN_ROWS, (r + 1) * FFN_ROWS)
            xb_ref[rows, :] = x_ref[rows, :].astype(BF16)
            o_ref[rows, :] = partial_out(rows)

    @pl.when(jnp.logical_and(f > 0, f < n_f - 1))
    def _():
        for r in range(FFN_TM // FFN_ROWS):
            rows = slice(r * FFN_ROWS, (r + 1) * FFN_ROWS)
            o_ref[rows, :] += partial_out(rows)

    @pl.when(f == n_f - 1)
    def _():
        for r in range(FFN_TM // FFN_LN_ROWS):
            rows = slice(r * FFN_LN_ROWS, (r + 1) * FFN_LN_ROWS)
            y = alpha * x_ref[rows, :] + FFN_RES_WEIGHT * (o_ref[rows, :] + partial_out(rows))
            o_ref[rows, :] = _layer_norm_rows(y, gain_ref[...], bias_ref[...])


def _ffn_ln(x, wg, wu, wd, gain, bias, alpha, later_ffn=None):
    m, d = x.shape
    n_f = D_FF // FFN_TF
    n_i = m // FFN_TM
    cast_args, cast_in_specs, cast_out_specs, cast_shapes = [], [], [], []
    if later_ffn is not None:
        lg, lu, ld, layer = later_ffn
        gu_block = (d // n_i, FFN_TF)
        dn_block = (D_FF // (n_i * n_f), d)
        gu_idx = lambda i, f: (i, f)
        dn_idx = lambda i, f: (i * n_f + f, 0)
        for w, block, idx in ((lg, gu_block, gu_idx), (lu, gu_block, gu_idx), (ld, dn_block, dn_idx)):
            cast_args.append(w)
            cast_in_specs.append(pl.BlockSpec((None,) + block, lambda i, f, idx=idx: (layer,) + idx(i, f)))
            cast_out_specs.append(pl.BlockSpec(block, idx))
            cast_shapes.append(jax.ShapeDtypeStruct(w.shape[1:], BF16))
    outs = pl.pallas_call(
        functools.partial(_ffn_ln_kernel, alpha=alpha, n_f=n_f, n_cast=len(cast_args)),
        grid=(n_i, n_f),
        in_specs=[
            pl.BlockSpec((FFN_TM, d), lambda i, f: (i, 0)),
            pl.BlockSpec((d, FFN_TF), lambda i, f: (0, f)),
            pl.BlockSpec((d, FFN_TF), lambda i, f: (0, f)),
            pl.BlockSpec((FFN_TF, d), lambda i, f: (f, 0)),
            pl.BlockSpec((1, d), lambda i, f: (0, 0)),
            pl.BlockSpec((1, d), lambda i, f: (0, 0)),
        ] + cast_in_specs,
        out_specs=[pl.BlockSpec((FFN_TM, d), lambda i, f: (i, 0))] + cast_out_specs,
        out_shape=[jax.ShapeDtypeStruct((m, d), F32)] + cast_shapes,
        scratch_shapes=[pltpu.VMEM((FFN_TM, d), BF16)],
        compiler_params=_cparams(("parallel", "arbitrary"), VMEM_LIMIT_FFN),
        name="ffn_ln",
    )(x, wg, wu, wd, gain, bias, *cast_args)
    return outs[0], tuple(outs[1:])


def _rope_tab_kernel(pos_ref, invf_ref, cos_ref, sin_ref):
    theta = pos_ref[...].astype(F32) * invf_ref[...]
    cos_ref[...] = jnp.cos(theta)
    sin_ref[...] = jnp.sin(theta)


def _rope_tables(pos_col, inv_freq):
    m = pos_col.shape[0]
    half = inv_freq.shape[1]
    return pl.pallas_call(
        _rope_tab_kernel,
        grid=(m // ROPE_TM,),
        in_specs=[
            pl.BlockSpec((ROPE_TM, 1), lambda i: (i, 0)),
            pl.BlockSpec((1, half), lambda i: (0, 0)),
        ],
        out_specs=[
            pl.BlockSpec((ROPE_TM, half), lambda i: (i, 0)),
            pl.BlockSpec((ROPE_TM, half), lambda i: (i, 0)),
        ],
        out_shape=[jax.ShapeDtypeStruct((m, half), F32)] * 2,
        compiler_params=_cparams(("parallel",)),
        name="rope_tab",
    )(pos_col, inv_freq)


def _in_proj_kernel(x_ref, wqk_ref, wrest_ref, wdt_ref, cos_ref, sin_ref, qk_ref, rest_ref, dt_ref, xb_ref,
                    *, n_qk, k_scale):
    j = pl.program_id(1)

    @pl.when(j == 0)
    def _():
        xb_ref[...] = x_ref[...].astype(BF16)
        dt_ref[...] = _dot(xb_ref[...], wdt_ref[...])

    @pl.when(j < n_qk)
    def _():
        acc = _dot(xb_ref[...], wqk_ref[...])
        scale = jnp.where(j >= n_qk // 2, k_scale, 1.0).astype(F32)
        c = cos_ref[...] * scale
        s = sin_ref[...] * scale
        half = RET_HEAD_DIM // 2
        for hd in range(acc.shape[1] // RET_HEAD_DIM):
            lo = hd * RET_HEAD_DIM
            e = acc[:, lo:lo + half]
            o = acc[:, lo + half:lo + RET_HEAD_DIM]
            qk_ref[:, lo:lo + half] = (e * c - o * s).astype(qk_ref.dtype)
            qk_ref[:, lo + half:lo + RET_HEAD_DIM] = (o * c + e * s).astype(qk_ref.dtype)

    @pl.when(j >= n_qk)
    def _():
        rest_ref[...] = _dot(xb_ref[...], wrest_ref[...]).astype(rest_ref.dtype)


def _in_proj(x, w_qk, w_rest, w_dt, cos, sin):
    m, k = x.shape
    half = RET_HEAD_DIM // 2
    tn_qk, tn_rest = PROJ_TN_QK, PROJ_TN_REST
    n_qk = w_qk.shape[1] // tn_qk
    n_rest = w_rest.shape[1] // tn_rest
    qk_j = lambda j: jnp.minimum(j, n_qk - 1)
    rest_j = lambda j: jnp.maximum(j - n_qk, 0)
    return pl.pallas_call(
        functools.partial(_in_proj_kernel, n_qk=n_qk, k_scale=RET_HEAD_DIM ** -0.5),
        grid=(m // PROJ_TM, n_qk + n_rest),
        in_specs=[
            pl.BlockSpec((PROJ_TM, k), lambda i, j: (i, 0)),
            pl.BlockSpec((k, tn_qk), lambda i, j: (0, qk_j(j))),
            pl.BlockSpec((k, tn_rest), lambda i, j: (0, rest_j(j))),
            pl.BlockSpec((k, LANES), lambda i, j: (0, 0)),
            pl.BlockSpec((PROJ_TM, half), lambda i, j: (i, 0)),
            pl.BlockSpec((PROJ_TM, half), lambda i, j: (i, 0)),
        ],
        out_specs=[
            pl.BlockSpec((PROJ_TM, tn_qk), lambda i, j: (i, qk_j(j))),
            pl.BlockSpec((PROJ_TM, tn_rest), lambda i, j: (i, rest_j(j))),
            pl.BlockSpec((PROJ_TM, LANES), lambda i, j: (i, 0)),
        ],
        out_shape=[
            jax.ShapeDtypeStruct((m, n_qk * tn_qk), BF16),
            jax.ShapeDtypeStruct((m, n_rest * tn_rest), BF16),
            jax.ShapeDtypeStruct((m, LANES), F32),
        ],
        scratch_shapes=[pltpu.VMEM((PROJ_TM, k), BF16)],
        compiler_params=_cparams(("parallel", "arbitrary")),
        name="in_proj",
    )(x, w_qk, w_rest, w_dt, cos, sin)


def _ret_log_gamma(h):
    return math.log(1.0 - 2.0 ** (-5.0 - h))


def _retention_heads(heads, q_ref, k_ref, v_ref, g_ref, dmask_ref, qd_ref, kd_ref, gain_ref, bias_ref,
                     o_ref, state_ref):
    tc = q_ref.shape[0]
    dh = RET_HEAD_DIM
    for h in heads:
        sl = slice(h * dh, (h + 1) * dh)
        q = q_ref[:, sl]
        k = k_ref[:, sl]
        v = v_ref[:, sl]
        scores = lax.dot_general(q, k, (((1,), (1,)), ((), ())), preferred_element_type=F32)
        p = (scores * dmask_ref[h]).astype(BF16)
        state = state_ref[h]
        o = _dot(p, v) + _dot(q, state.astype(BF16)) * qd_ref[:, sl]
        vkd = (v.astype(F32) * kd_ref[:, sl]).astype(BF16)
        kv = lax.dot_general(k, vkd, (((0,), (0,)), ((), ())), preferred_element_type=F32)
        state_ref[h] = state * math.exp(_ret_log_gamma(h) * tc) + kv
        mu = jnp.mean(o, axis=-1, keepdims=True)
        d = o - mu
        var = jnp.mean(d * d, axis=-1, keepdims=True)
        on = d * lax.rsqrt(var + LN_EPS) * gain_ref[:, sl] + bias_ref[:, sl]
        o_ref[:, sl] = (_silu(g_ref[:, sl].astype(F32)) * on).astype(o_ref.dtype)


def _retention_consts(tc):
    lg = jnp.asarray([_ret_log_gamma(h) for h in range(RET_HEADS)], F32)
    pos = jnp.arange(tc, dtype=F32)
    rel = pos[:, None] - pos[None, :]
    dmask = jnp.where(rel >= 0, jnp.exp(lg[:, None, None] * jnp.maximum(rel, 0.0)), 0.0)
    qd = jnp.exp(lg[None, :] * (pos[:, None] + 1.0))
    kd = jnp.exp(lg[None, :] * (tc - 1.0 - pos[:, None]))
    rep = lambda t: jnp.repeat(t, RET_HEAD_DIM, axis=1)
    return dmask.astype(F32), rep(qd).astype(F32), rep(kd).astype(F32)


def _ssd_conv(xbc_ref, convw_ref, convb_ref, xpad_ref, xc_ref):
    t = xbc_ref.shape[0]
    xf = xbc_ref[...].astype(F32)
    xpad_ref[SUBLANES:, :] = xf
    conv = convb_ref[...] + convw_ref[CONV_WIDTH - 1:CONV_WIDTH, :] * xf
    for s in range(1, CONV_WIDTH):
        w_s = convw_ref[CONV_WIDTH - 1 - s:CONV_WIDTH - s, :]
        conv = conv + w_s * xpad_ref[SUBLANES - s:SUBLANES - s + t, :]
    xc_ref[...] = _silu(conv)
    xpad_ref[0:SUBLANES, :] = xf[t - SUBLANES:t, :]


def _ssd_chunks(chunks, z_ref, dtr_ref, dtb_ref, alog_ref, dexp_ref, ngain_ref, expand_ref, o_ref, state_ref,
                xc_ref):
    ts = SSD_CHUNK
    n = SSD_STATE
    gw = SSD_WIDTH // SSD_GROUPS
    ri = lax.broadcasted_iota(jnp.int32, (ts, ts), 0)
    ci = lax.broadcasted_iota(jnp.int32, (ts, ts), 1)
    causal = ri >= ci
    tri = jnp.where(causal, 1.0, 0.0).astype(BF16)
    lane_lo = ci < SSD_HEAD_DIM
    expand = expand_ref[...]
    a_neg = -jnp.exp(alog_ref[...])

    for j in chunks:
        rows = slice(j * ts, (j + 1) * ts)
        xs = xc_ref[rows, 0:SSD_WIDTH]
        bm = xc_ref[rows, SSD_WIDTH:SSD_WIDTH + SSD_GROUPS * n].astype(BF16)
        cm = xc_ref[rows, SSD_WIDTH + SSD_GROUPS * n:CONV_CH].astype(BF16)

        pre = dtr_ref[rows, :] + dtb_ref[...]
        dt = jnp.maximum(pre, 0.0) + jnp.log1p(jnp.exp(-jnp.abs(pre)))
        da = dt * a_neg
        acum = _dot_exact01(tri, da, left=True)
        acum_t = acum.T
        acum_e = _dot_exact01(expand, acum, left=False)
        dt_e = _dot_exact01(expand, dt, left=False)
        last_e = acum_e[ts - 1:ts, :]
        from_start = jnp.exp(acum_e)
        to_end = jnp.exp(last_e - acum_e)
        total = jnp.exp(last_e)

        xdt = xs * dt_e
        xdt_te = (xdt * to_end).astype(BF16)

        zf = z_ref[rows, :].astype(F32)
        gate = _silu(zf)
        for g in range(SSD_GROUPS):
            cg = cm[:, g * n:(g + 1) * n]
            bg = bm[:, g * n:(g + 1) * n]
            cb = lax.dot_general(cg, bg, (((1,), (1,)), ((), ())), preferred_element_type=F32)
            intra = []
            for pr in range(gw // LANES):
                lanes = slice(g * gw + pr * LANES, g * gw + (pr + 1) * LANES)
                xpair = xdt[:, lanes]
                y_pair = None
                for half in range(2):
                    hd = (g * gw + pr * LANES) // SSD_HEAD_DIM + half
                    seg = acum[:, hd:hd + 1] - acum_t[hd:hd + 1, :]
                    decay = jnp.exp(jnp.where(causal, seg, -1e30))
                    wmat = (cb * decay).astype(BF16)
                    keep = lane_lo if half == 0 else jnp.logical_not(lane_lo)
                    xh = jnp.where(keep, xpair, 0.0).astype(BF16)
                    term = _dot(wmat, xh)
                    y_pair = term if y_pair is None else y_pair + term
                intra.append(y_pair)
            y_intra = jnp.concatenate(intra, axis=1)
            gl = slice(g * gw, (g + 1) * gw)
            st = state_ref[:, gl]
            y_inter = _dot(cg, st.astype(BF16)) * from_start[:, gl]
            upd = lax.dot_general(bg, xdt_te[:, gl], (((0,), (0,)), ((), ())), preferred_element_type=F32)
            state_ref[:, gl] = st * total[:, gl] + upd
            y = (y_intra + y_inter + xs[:, gl] * dexp_ref[:, gl]) * gate[:, gl]
            ms = jnp.mean(y * y, axis=-1, keepdims=True)
            o_ref[rows, gl] = (y * lax.rsqrt(ms + LN_EPS) * ngain_ref[:, gl]).astype(o_ref.dtype)


N_RET_IN = 9
N_SSD_IN = 10


def _mixer_kernel(*refs):
    ret_in = refs[:N_RET_IN]
    ssd_in = refs[N_RET_IN:N_RET_IN + N_SSD_IN]
    ret_o, ssd_o, ret_state, xpad_ref, ssd_state, xc_ref = refs[N_RET_IN + N_SSD_IN:]

    @pl.when(pl.program_id(1) == 0)
    def _():
        ret_state[...] = jnp.zeros_like(ret_state)
        xpad_ref[0:SUBLANES, :] = jnp.zeros((SUBLANES, CONV_CH), F32)
        ssd_state[...] = jnp.zeros_like(ssd_state)

    z_ref, xbc_ref, dtr_ref, convw_ref, convb_ref, dtb_ref, alog_ref, dexp_ref, ngain_ref, expand_ref = ssd_in
    _retention_heads(range(RET_HEADS), *ret_in, ret_o, ret_state)
    _ssd_conv(xbc_ref, convw_ref, convb_ref, xpad_ref, xc_ref)
    _ssd_chunks(range(xbc_ref.shape[0] // SSD_CHUNK), z_ref, dtr_ref, dtb_ref, alog_ref, dexp_ref, ngain_ref,
                expand_ref, ssd_o, ssd_state, xc_ref)


def _mixer(qk, rest, dtr, dmask, qd, kd, ret_gain, ret_bias, convw, convb, dtb, a_log, dexp, ngain, expand,
           batch, seq):
    t = MIX_STEP
    nc = seq // t
    w = RET_WIDTH
    row = lambda b, c: b * nc + c
    const2 = lambda b, c: (0, 0)
    return pl.pallas_call(
        _mixer_kernel,
        grid=(batch, nc),
        in_specs=[
            pl.BlockSpec((t, w), lambda b, c: (row(b, c), 0)),
            pl.BlockSpec((t, w), lambda b, c: (row(b, c), 1)),
            pl.BlockSpec((t, w), lambda b, c: (row(b, c), 0)),
            pl.BlockSpec((t, w), lambda b, c: (row(b, c), 1)),
            pl.BlockSpec((RET_HEADS, t, t), lambda b, c: (0, 0, 0)),
            pl.BlockSpec((t, w), const2),
            pl.BlockSpec((t, w), const2),
            pl.BlockSpec((1, w), const2),
            pl.BlockSpec((1, w), const2),
            pl.BlockSpec((t, SSD_WIDTH), lambda b, c: (row(b, c), 2)),
            pl.BlockSpec((t, CONV_CH), lambda b, c: (row(b, c), 2)),
            pl.BlockSpec((t, LANES), lambda b, c: (row(b, c), 0)),
            pl.BlockSpec((CONV_WIDTH, CONV_CH), const2),
            pl.BlockSpec((1, CONV_CH), const2),
            pl.BlockSpec((1, LANES), const2),
            pl.BlockSpec((1, LANES), const2),
            pl.BlockSpec((1, SSD_WIDTH), const2),
            pl.BlockSpec((1, SSD_WIDTH), const2),
            pl.BlockSpec((LANES, SSD_WIDTH), const2),
        ],
        out_specs=[
            pl.BlockSpec((t, w), lambda b, c: (row(b, c), 0)),
            pl.BlockSpec((t, SSD_WIDTH), lambda b, c: (row(b, c), 0)),
        ],
        out_shape=[
            jax.ShapeDtypeStruct((batch * seq, w), BF16),
            jax.ShapeDtypeStruct((batch * seq, SSD_WIDTH), BF16),
        ],
        scratch_shapes=[
            pltpu.VMEM((RET_HEADS, RET_HEAD_DIM, RET_HEAD_DIM), F32),
            pltpu.VMEM((t + SUBLANES, CONV_CH), F32),
            pltpu.VMEM((SSD_STATE, SSD_WIDTH), F32),
            pltpu.VMEM((t, CONV_CH), F32),
        ],
        compiler_params=_cparams(("parallel", "arbitrary")),
        name="mixer",
    )(qk, qk, rest, rest, dmask, qd, kd, ret_gain, ret_bias,
      rest, rest, dtr, convw, convb, dtb, a_log, dexp, ngain, expand)


def _outproj_ln_kernel(ret_ref, ssd_ref, x_ref, wo_ref, gain_ref, bias_ref, o_ref, *, alpha):
    for r in range(OUT_TM // OUT_ROWS):
        rows = slice(r * OUT_ROWS, (r + 1) * OUT_ROWS)
        mix = _dot(ret_ref[rows, :], wo_ref[0:RET_WIDTH, :]) + _dot(ssd_ref[rows, :], wo_ref[RET_WIDTH:, :])
        y = alpha * x_ref[rows, :] + mix
        o_ref[rows, :] = _layer_norm_rows(y, gain_ref[...], bias_ref[...])


def _outproj_ln(ret, ssd, x, wo, gain, bias, alpha):
    m, d = x.shape
    return pl.pallas_call(
        functools.partial(_outproj_ln_kernel, alpha=alpha),
        grid=(m // OUT_TM,),
        in_specs=[
            pl.BlockSpec((OUT_TM, RET_WIDTH), lambda i: (i, 0)),
            pl.BlockSpec((OUT_TM, SSD_WIDTH), lambda i: (i, 0)),
            pl.BlockSpec((OUT_TM, d), lambda i: (i, 0)),
            pl.BlockSpec((RET_WIDTH + SSD_WIDTH, d), lambda i: (0, 0), pipeline_mode=pl.Buffered(1)),
            pl.BlockSpec((1, d), lambda i: (0, 0)),
            pl.BlockSpec((1, d), lambda i: (0, 0)),
        ],
        out_specs=pl.BlockSpec((OUT_TM, d), lambda i: (i, 0)),
        out_shape=jax.ShapeDtypeStruct((m, d), F32),
        compiler_params=_cparams(("parallel",)),
        name="outproj_ln",
    )(ret, ssd, x, wo, gain, bias)


def _w_in_prep_kernel(w_ref, perm_ref, qk_ref, rest_ref, dt_ref):
    n_qk = qk_ref.shape[1]
    n_rest = rest_ref.shape[1]
    for hd in range(n_qk // RET_HEAD_DIM):
        cols = slice(hd * RET_HEAD_DIM, (hd + 1) * RET_HEAD_DIM)
        qk_ref[:, cols] = _dot(w_ref[:, cols].astype(BF16), perm_ref[...]).astype(BF16)
    rest_ref[...] = w_ref[:, n_qk:n_qk + n_rest].astype(BF16)
    dt_ref[...] = jnp.zeros(dt_ref.shape, dt_ref.dtype)
    dt_ref[:, 0:SSD_HEADS] = w_ref[:, n_qk + n_rest:n_qk + n_rest + SSD_HEADS].astype(BF16)


def _w_in_prep(w_in_layers, layer, perm):
    _, k, n = w_in_layers.shape
    n_qk = 2 * RET_WIDTH
    n_rest = 2 * RET_WIDTH + SSD_WIDTH + CONV_CH
    return pl.pallas_call(
        _w_in_prep_kernel,
        grid=(k // W_PREP_ROWS,),
        in_specs=[
            pl.BlockSpec((None, W_PREP_ROWS, n), lambda i: (layer, i, 0)),
            pl.BlockSpec((RET_HEAD_DIM, RET_HEAD_DIM), lambda i: (0, 0)),
        ],
        out_specs=[
            pl.BlockSpec((W_PREP_ROWS, n_qk), lambda i: (i, 0)),
            pl.BlockSpec((W_PREP_ROWS, n_rest), lambda i: (i, 0)),
            pl.BlockSpec((W_PREP_ROWS, LANES), lambda i: (i, 0)),
        ],
        out_shape=[
            jax.ShapeDtypeStruct((k, n_qk), BF16),
            jax.ShapeDtypeStruct((k, n_rest), BF16),
            jax.ShapeDtypeStruct((k, LANES), BF16),
        ],
        compiler_params=_cparams(("parallel",)),
        name="w_in_prep",
    )(w_in_layers, perm)


def _even_odd_permutation():
    half = RET_HEAD_DIM // 2
    j = jnp.arange(RET_HEAD_DIM)
    src = jnp.where(j < half, 2 * j, 2 * (j - half) + 1)
    return (jnp.arange(RET_HEAD_DIM)[:, None] == src[None, :]).astype(BF16)


def kernel(x, positions, ffn1_w_gate, ffn1_w_up, ffn1_w_down, ln1_gain, ln1_bias, mix_w_in, ret_gn_gain, ret_gn_bias, ssd_conv_w, ssd_conv_b, ssd_dt_bias, ssd_a_log, ssd_d, ssd_norm_gain, mix_w_out, ln2_gain, ln2_bias, ffn2_w_gate, ffn2_w_up, ffn2_w_down, ln3_gain, ln3_bias):
    batch, seq, d = x.shape
    depth = ffn1_w_gate.shape[0]
    alpha = (2.0 * depth) ** 0.25
    m = batch * seq
    h = x.reshape(m, d)

    half = RET_HEAD_DIM // 2
    inv_freq = (1.0 / (ROPE_BASE ** jnp.linspace(0.0, 1.0, half, dtype=F32))).reshape(1, half)
    cos, sin = _rope_tables(positions.reshape(m, 1), inv_freq)
    dmask, qd, kd = _retention_consts(RET_CHUNK)
    perm = _even_odd_permutation()
    expand = (jnp.arange(SSD_WIDTH)[None, :] // SSD_HEAD_DIM == jnp.arange(LANES)[:, None]).astype(BF16)
    row = lambda v: v.reshape(1, -1).astype(F32)
    pad_lanes = lambda v: jnp.pad(v.astype(F32), (0, LANES - v.shape[0])).reshape(1, LANES)

    for layer in range(depth):
        w_qk, w_rest, w_dt = _w_in_prep(mix_w_in, layer, perm)

        h, ffn2_w = _ffn_ln(h, ffn1_w_gate[layer].astype(BF16), ffn1_w_up[layer].astype(BF16),
                            ffn1_w_down[layer].astype(BF16), row(ln1_gain[layer]), row(ln1_bias[layer]), alpha,
                            later_ffn=(ffn2_w_gate, ffn2_w_up, ffn2_w_down, layer))

        qk, rest, dtr = _in_proj(h, w_qk, w_rest, w_dt, cos, sin)

        dexp = jnp.repeat(ssd_d[layer].astype(F32), SSD_HEAD_DIM).reshape(1, SSD_WIDTH)
        ret, ssd = _mixer(qk, rest, dtr, dmask, qd, kd, row(ret_gn_gain[layer]), row(ret_gn_bias[layer]),
                          ssd_conv_w[layer].astype(F32), row(ssd_conv_b[layer]), pad_lanes(ssd_dt_bias[layer]),
                          pad_lanes(ssd_a_log[layer]), dexp, row(ssd_norm_gain[layer]), expand, batch, seq)

        h = _outproj_ln(ret, ssd, h, mix_w_out[layer].astype(BF16), row(ln2_gain[layer]),
                        row(ln2_bias[layer]), alpha)

        h, _ = _ffn_ln(h, *ffn2_w, row(ln3_gain[layer]), row(ln3_bias[layer]), alpha)

    return h.reshape(batch, seq, d)
```

```python
import functools
import math

import jax
import jax.numpy as jnp
from jax import lax
from jax.experimental import pallas as pl
from jax.experimental.pallas import tpu as pltpu

F32 = jnp.float32
BF16 = jnp.bfloat16

D_MODEL = 2048
RET_WIDTH = 1024
RET_HEAD_DIM = 256
RET_HEADS = 4
SSD_WIDTH = 1024
SSD_HEAD_DIM = 64
SSD_HEADS = 16
SSD_GROUPS = 2
SSD_STATE = 128
CONV_WIDTH = 4
CONV_CH = SSD_WIDTH + 2 * SSD_GROUPS * SSD_STATE
D_FF = 5632
ROPE_BASE = 10000.0
LN_EPS = 1e-5
FFN_RES_WEIGHT = 0.5

LANES = 128
SUBLANES = 8
VMEM_BYTES_V7X = 64 * 1024 * 1024
VMEM_LIMIT = VMEM_BYTES_V7X - 4 * 1024 * 1024
VMEM_LIMIT_FFN = VMEM_BYTES_V7X - 1024 * 1024

FFN_TM = 1024
FFN_TF = 512
FFN_ROWS = 512
FFN_LN_ROWS = 256
PROJ_TM = 1024
PROJ_TN_QK = 1024
PROJ_TN_REST = 768
OUT_TM = 1024
OUT_ROWS = 128
RET_CHUNK = 256
SSD_CHUNK = 128
MIX_STEP = 256
ROPE_TM = 1024
W_PREP_ROWS = 256


def _cparams(sem, vmem_limit=VMEM_LIMIT):
    return pltpu.CompilerParams(dimension_semantics=sem, vmem_limit_bytes=vmem_limit)


def _silu(v):
    return v / (1.0 + jnp.exp(-v))


def _layer_norm_rows(y, gain, bias):
    mu = jnp.mean(y, axis=-1, keepdims=True)
    d = y - mu
    var = jnp.mean(d * d, axis=-1, keepdims=True)
    return d * lax.rsqrt(var + LN_EPS) * gain + bias


def _split3(v):
    h0 = v.astype(BF16)
    r1 = v - h0.astype(F32)
    h1 = r1.astype(BF16)
    h2 = (r1 - h1.astype(F32)).astype(BF16)
    return h0, h1, h2


def _dot(a, b):
    return jnp.dot(a, b, preferred_element_type=F32)


def _dot_exact01(m01, v, left=True):
    parts = _split3(v)
    if left:
        return _dot(m01, parts[0]) + _dot(m01, parts[1]) + _dot(m01, parts[2])
    return _dot(parts[0], m01) + _dot(parts[1], m01) + _dot(parts[2], m01)


def _ffn_ln_kernel(*refs, alpha, n_f, n_cast):
    x_ref, wg_ref, wu_ref, wd_ref, gain_ref, bias_ref = refs[:6]
    cast_in = refs[6:6 + n_cast]
    o_ref = refs[6 + n_cast]
    cast_out = refs[7 + n_cast:7 + 2 * n_cast]
    xb_ref = refs[7 + 2 * n_cast]
    f = pl.program_id(1)

    for src, dst in zip(cast_in, cast_out):
        dst[...] = src[...].astype(dst.dtype)

    def partial_out(rows):
        xb = xb_ref[rows, :]
        h = (_silu(_dot(xb, wg_ref[...])) * _dot(xb, wu_ref[...])).astype(BF16)
        return _dot(h, wd_ref[...])

    @pl.when(f == 0)
    def _():
        for r in range(FFN_TM // FFN_ROWS):
            rows = slice(r * FFN_ROWS, (r + 1) * FFN_ROWS)
            xb_ref[rows, :] = x_ref[rows, :].astype(BF16)
            o_ref[rows, :] = partial_out(rows)

    @pl.when(jnp.logical_and(f > 0, f < n_f - 1))
    def _():
        for r in range(FFN_TM // FFN_ROWS):
            rows = slice(r * FFN_ROWS, (r + 1) * FFN_ROWS)
            o_ref[rows, :] += partial_out(rows)

    @pl.when(f == n_f - 1)
    def _():
        for r in range(FFN_TM // FFN_LN_ROWS):
            rows = slice(r * FFN_LN_ROWS, (r + 1) * FFN_LN_ROWS)
            y = alpha * x_ref[rows, :] + FFN_RES_WEIGHT * (o_ref[rows, :] + partial_out(rows))
            o_ref[rows, :] = _layer_norm_rows(y, gain_ref[...], bias_ref[...])


def _ffn_ln(x, wg, wu, wd, gain, bias, alpha, later_ffn=None):
    m, d = x.shape
    n_f = D_FF // FFN_TF
    n_i = m // FFN_TM
    cast_args, cast_in_specs, cast_out_specs, cast_shapes = [], [], [], []
    if later_ffn is not None:
        lg, lu, ld, layer = later_ffn
        gu_block = (d // n_i, FFN_TF)
        dn_block = (D_FF // (n_i * n_f), d)
        gu_idx = lambda i, f: (i, f)
        dn_idx = lambda i, f: (i * n_f + f, 0)
        for w, block, idx in ((lg, gu_block, gu_idx), (lu, gu_block, gu_idx), (ld, dn_block, dn_idx)):
            cast_args.append(w)
            cast_in_specs.append(pl.BlockSpec((None,) + block, lambda i, f, idx=idx: (layer,) + idx(i, f)))
            cast_out_specs.append(pl.BlockSpec(block, idx))
            cast_shapes.append(jax.ShapeDtypeStruct(w.shape[1:], BF16))
    outs = pl.pallas_call(
        functools.partial(_ffn_ln_kernel, alpha=alpha, n_f=n_f, n_cast=len(cast_args)),
        grid=(n_i, n_f),
        in_specs=[
            pl.BlockSpec((FFN_TM, d), lambda i, f: (i, 0)),
            pl.BlockSpec((d, FFN_TF), lambda i, f: (0, f)),
            pl.BlockSpec((d, FFN_TF), lambda i, f: (0, f)),
            pl.BlockSpec((FFN_TF, d), lambda i, f: (f, 0)),
            pl.BlockSpec((1, d), lambda i, f: (0, 0)),
            pl.BlockSpec((1, d), lambda i, f: (0, 0)),
        ] + cast_in_specs,
        out_specs=[pl.BlockSpec((FFN_TM, d), lambda i, f: (i, 0))] + cast_out_specs,
        out_shape=[jax.ShapeDtypeStruct((m, d), F32)] + cast_shapes,
        scratch_shapes=[pltpu.VMEM((FFN_TM, d), BF16)],
        compiler_params=_cparams(("parallel", "arbitrary"), VMEM_LIMIT_FFN),
        name="ffn_ln",
    )(x, wg, wu, wd, gain, bias, *cast_args)
    return outs[0], tuple(outs[1:])


def _rope_tab_kernel(pos_ref, invf_ref, cos_ref, sin_ref):
    theta = pos_ref[...].astype(F32) * invf_ref[...]
    cos_ref[...] = jnp.cos(theta)
    sin_ref[...] = jnp.sin(theta)


def _rope_tables(pos_col, inv_freq):
    m = pos_col.shape[0]
    half = inv_freq.shape[1]
    return pl.pallas_call(
        _rope_tab_kernel,
        grid=(m // ROPE_TM,),
        in_specs=[
            pl.BlockSpec((ROPE_TM, 1), lambda i: (i, 0)),
            pl.BlockSpec((1, half), lambda i: (0, 0)),
        ],
        out_specs=[
            pl.BlockSpec((ROPE_TM, half), lambda i: (i, 0)),
            pl.BlockSpec((ROPE_TM, half), lambda i: (i, 0)),
        ],
        out_shape=[jax.ShapeDtypeStruct((m, half), F32)] * 2,
        compiler_params=_cparams(("parallel",)),
        name="rope_tab",
    )(pos_col, inv_freq)


def _in_proj_kernel(xa_ref, xb_half_ref, wqk_ref, wrest_ref, wdt_ref, cos_ref, sin_ref, qk_ref, rest_ref, dt_ref,
                    xb_ref, *, n_qk, k_scale):
    j = pl.program_id(1)

    @pl.when(j == 0)
    def _():
        hr = xa_ref.shape[0]
        xb_ref[0:hr, :] = xa_ref[...].astype(BF16)
        xb_ref[hr:, :] = xb_half_ref[...].astype(BF16)
        dt_ref[...] = _dot(xb_ref[...], wdt_ref[...])

    @pl.when(j < n_qk)
    def _():
        acc = _dot(xb_ref[...], wqk_ref[...])
        scale = jnp.where(j >= n_qk // 2, k_scale, 1.0).astype(F32)
        c = cos_ref[...] * scale
        s = sin_ref[...] * scale
        half = RET_HEAD_DIM // 2
        for hd in range(acc.shape[1] // RET_HEAD_DIM):
            lo = hd * RET_HEAD_DIM
            e = acc[:, lo:lo + half]
            o = acc[:, lo + half:lo + RET_HEAD_DIM]
            qk_ref[:, lo:lo + half] = (e * c - o * s).astype(qk_ref.dtype)
            qk_ref[:, lo + half:lo + RET_HEAD_DIM] = (o * c + e * s).astype(qk_ref.dtype)

    @pl.when(j >= n_qk)
    def _():
        rest_ref[...] = _dot(xb_ref[...], wrest_ref[...]).astype(rest_ref.dtype)


def _in_proj(x, w_qk, w_rest, w_dt, cos, sin):
    m, k = x.shape
    half = RET_HEAD_DIM // 2
    tn_qk, tn_rest = PROJ_TN_QK, PROJ_TN_REST
    n_qk = w_qk.shape[1] // tn_qk
    n_rest = w_rest.shape[1] // tn_rest
    qk_j = lambda j: jnp.minimum(j, n_qk - 1)
    rest_j = lambda j: jnp.maximum(j - n_qk, 0)
    n_i = m // PROJ_TM
    n_j = n_qk + n_rest
    half_rows = lambda h, switch: (lambda i, j: (2 * jnp.minimum(i + (j >= switch), n_i - 1) + h, 0))
    return pl.pallas_call(
        functools.partial(_in_proj_kernel, n_qk=n_qk, k_scale=RET_HEAD_DIM ** -0.5),
        grid=(n_i, n_j),
        in_specs=[
            pl.BlockSpec((PROJ_TM // 2, k), half_rows(0, n_j // 3)),
            pl.BlockSpec((PROJ_TM // 2, k), half_rows(1, (2 * n_j) // 3)),
            pl.BlockSpec((k, tn_qk), lambda i, j: (0, qk_j(j))),
            pl.BlockSpec((k, tn_rest), lambda i, j: (0, rest_j(j))),
            pl.BlockSpec((k, LANES), lambda i, j: (0, 0)),
            pl.BlockSpec((PROJ_TM, half), lambda i, j: (i, 0)),
            pl.BlockSpec((PROJ_TM, half), lambda i, j: (i, 0)),
        ],
        out_specs=[
            pl.BlockSpec((PROJ_TM, tn_qk), lambda i, j: (i, qk_j(j))),
            pl.BlockSpec((PROJ_TM, tn_rest), lambda i, j: (i, rest_j(j))),
            pl.BlockSpec((PROJ_TM, LANES), lambda i, j: (i, 0)),
        ],
        out_shape=[
            jax.ShapeDtypeStruct((m, n_qk * tn_qk), BF16),
            jax.ShapeDtypeStruct((m, n_rest * tn_rest), BF16),
            jax.ShapeDtypeStruct((m, LANES), F32),
        ],
        scratch_shapes=[pltpu.VMEM((PROJ_TM, k), BF16)],
        compiler_params=_cparams(("parallel", "arbitrary")),
        name="in_proj",
    )(x, x, w_qk, w_rest, w_dt, cos, sin)


def _ret_log_gamma(h):
    return math.log(1.0 - 2.0 ** (-5.0 - h))


def _retention_heads(heads, q_ref, k_ref, v_ref, g_ref, dmask_ref, qd_ref, kd_ref, gain_ref, bias_ref,
                     o_ref, state_ref):
    tc = q_ref.shape[0]
    dh = RET_HEAD_DIM
    for h in heads:
        sl = slice(h * dh, (h + 1) * dh)
        q = q_ref[:, sl]
        k = k_ref[:, sl]
        v = v_ref[:, sl]
        scores = lax.dot_general(q, k, (((1,), (1,)), ((), ())), preferred_element_type=F32)
        p = (scores * dmask_ref[h]).astype(BF16)
        state = state_ref[h]
        o = _dot(p, v) + _dot(q, state.astype(BF16)) * qd_ref[:, sl]
        vkd = (v.astype(F32) * kd_ref[:, sl]).astype(BF16)
        kv = lax.dot_general(k, vkd, (((0,), (0,)), ((), ())), preferred_element_type=F32)
        state_ref[h] = state * math.exp(_ret_log_gamma(h) * tc) + kv
        mu = jnp.mean(o, axis=-1, keepdims=True)
        d = o - mu
        var = jnp.mean(d * d, axis=-1, keepdims=True)
        on = d * lax.rsqrt(var + LN_EPS) * gain_ref[:, sl] + bias_ref[:, sl]
        o_ref[:, sl] = (_silu(g_ref[:, sl].astype(F32)) * on).astype(o_ref.dtype)


def _retention_consts(tc):
    lg = jnp.asarray([_ret_log_gamma(h) for h in range(RET_HEADS)], F32)
    pos = jnp.arange(tc, dtype=F32)
    rel = pos[:, None] - pos[None, :]
    dmask = jnp.where(rel >= 0, jnp.exp(lg[:, None, None] * jnp.maximum(rel, 0.0)), 0.0)
    qd = jnp.exp(lg[None, :] * (pos[:, None] + 1.0))
    kd = jnp.exp(lg[None, :] * (tc - 1.0 - pos[:, None]))
    rep = lambda t: jnp.repeat(t, RET_HEAD_DIM, axis=1)
    return dmask.astype(F32), rep(qd).astype(F32), rep(kd).astype(F32)


def _ssd_conv(xbc_ref, convw_ref, convb_ref, xpad_ref, xc_ref):
    t = xbc_ref.shape[0]
    xf = xbc_ref[...].astype(F32)
    xpad_ref[SUBLANES:, :] = xf
    conv = convb_ref[...] + convw_ref[CONV_WIDTH - 1:CONV_WIDTH, :] * xf
    for s in range(1, CONV_WIDTH):
        w_s = convw_ref[CONV_WIDTH - 1 - s:CONV_WIDTH - s, :]
        conv = conv + w_s * xpad_ref[SUBLANES - s:SUBLANES - s + t, :]
    xc_ref[...] = _silu(conv)
    xpad_ref[0:SUBLANES, :] = xf[t - SUBLANES:t, :]


def _ssd_chunks(chunks, z_ref, dtr_ref, dtb_ref, alog_ref, dexp_ref, ngain_ref, expand_ref, o_ref, state_ref,
                xc_ref):
    ts = SSD_CHUNK
    n = SSD_STATE
    gw = SSD_WIDTH // SSD_GROUPS
    ri = lax.broadcasted_iota(jnp.int32, (ts, ts), 0)
    ci = lax.broadcasted_iota(jnp.int32, (ts, ts), 1)
    causal = ri >= ci
    tri = jnp.where(causal, 1.0, 0.0).astype(BF16)
    lane_lo = ci < SSD_HEAD_DIM
    expand = expand_ref[...]
    a_neg = -jnp.exp(alog_ref[...])

    for j in chunks:
        rows = slice(j * ts, (j + 1) * ts)
        xs = xc_ref[rows, 0:SSD_WIDTH]
        bm = xc_ref[rows, SSD_WIDTH:SSD_WIDTH + SSD_GROUPS * n].astype(BF16)
        cm = xc_ref[rows, SSD_WIDTH + SSD_GROUPS * n:CONV_CH].astype(BF16)

        pre = dtr_ref[rows, :] + dtb_ref[...]
        dt = jnp.maximum(pre, 0.0) + jnp.log1p(jnp.exp(-jnp.abs(pre)))
        da = dt * a_neg
        acum = _dot_exact01(tri, da, left=True)
        acum_t = acum.T
        acum_e = _dot_exact01(expand, acum, left=False)
        dt_e = _dot_exact01(expand, dt, left=False)
        last_e = acum_e[ts - 1:ts, :]
        from_start = jnp.exp(acum_e)
        to_end = jnp.exp(last_e - acum_e)
        total = jnp.exp(last_e)

        xdt = xs * dt_e
        xdt_te = (xdt * to_end).astype(BF16)

        zf = z_ref[rows, :].astype(F32)
        gate = _silu(zf)
        for g in range(SSD_GROUPS):
            cg = cm[:, g * n:(g + 1) * n]
            bg = bm[:, g * n:(g + 1) * n]
            cb = lax.dot_general(cg, bg, (((1,), (1,)), ((), ())), preferred_element_type=F32)
            intra = []
            for pr in range(gw // LANES):
                lanes = slice(g * gw + pr * LANES, g * gw + (pr + 1) * LANES)
                xpair = xdt[:, lanes]
                y_pair = None
                for half in range(2):
                    hd = (g * gw + pr * LANES) // SSD_HEAD_DIM + half
                    seg = acum[:, hd:hd + 1] - acum_t[hd:hd + 1, :]
                    decay = jnp.exp(jnp.where(causal, seg, -1e30))
                    wmat = (cb * decay).astype(BF16)
                    keep = lane_lo if half == 0 else jnp.logical_not(lane_lo)
                    xh = jnp.where(keep, xpair, 0.0).astype(BF16)
                    term = _dot(wmat, xh)
                    y_pair = term if y_pair is None else y_pair + term
                intra.append(y_pair)
            y_intra = jnp.concatenate(intra, axis=1)
            gl = slice(g * gw, (g + 1) * gw)
            st = state_ref[:, gl]
            y_inter = _dot(cg, st.astype(BF16)) * from_start[:, gl]
            upd = lax.dot_general(bg, xdt_te[:, gl], (((0,), (0,)), ((), ())), preferred_element_type=F32)
            state_ref[:, gl] = st * total[:, gl] + upd
            y = (y_intra + y_inter + xs[:, gl] * dexp_ref[:, gl]) * gate[:, gl]
            ms = jnp.mean(y * y, axis=-1, keepdims=True)
            o_ref[rows, gl] = (y * lax.rsqrt(ms + LN_EPS) * ngain_ref[:, gl]).astype(o_ref.dtype)


N_RET_IN = 9
N_SSD_IN = 10


def _mixer_kernel(*refs):
    ret_in = refs[:N_RET_IN]
    ssd_in = refs[N_RET_IN:N_RET_IN + N_SSD_IN]
    ret_o, ssd_o, ret_state, xpad_ref, ssd_state, xc_ref = refs[N_RET_IN + N_SSD_IN:]

    @pl.when(pl.program_id(1) == 0)
    def _():
        ret_state[...] = jnp.zeros_like(ret_state)
        xpad_ref[0:SUBLANES, :] = jnp.zeros((SUBLANES, CONV_CH), F32)
        ssd_state[...] = jnp.zeros_like(ssd_state)

    z_ref, xbc_ref, dtr_ref, convw_ref, convb_ref, dtb_ref, alog_ref, dexp_ref, ngain_ref, expand_ref = ssd_in
    _retention_heads(range(RET_HEADS), *ret_in, ret_o, ret_state)
    _ssd_conv(xbc_ref, convw_ref, convb_ref, xpad_ref, xc_ref)
    _ssd_chunks(range(xbc_ref.shape[0] // SSD_CHUNK), z_ref, dtr_ref, dtb_ref, alog_ref, dexp_ref, ngain_ref,
                expand_ref, ssd_o, ssd_state, xc_ref)


def _mixer(qk, rest, dtr, dmask, qd, kd, ret_gain, ret_bias, convw, convb, dtb, a_log, dexp, ngain, expand,
           batch, seq):
    t = MIX_STEP
    nc = seq // t
    w = RET_WIDTH
    row = lambda b, c: b * nc + c
    const2 = lambda b, c: (0, 0)
    return pl.pallas_call(
        _mixer_kernel,
        grid=(batch, nc),
        in_specs=[
            pl.BlockSpec((t, w), lambda b, c: (row(b, c), 0)),
            pl.BlockSpec((t, w), lambda b, c: (row(b, c), 1)),
            pl.BlockSpec((t, w), lambda b, c: (row(b, c), 0)),
            pl.BlockSpec((t, w), lambda b, c: (row(b, c), 1)),
            pl.BlockSpec((RET_HEADS, t, t), lambda b, c: (0, 0, 0)),
            pl.BlockSpec((t, w), const2),
            pl.BlockSpec((t, w), const2),
            pl.BlockSpec((1, w), const2),
            pl.BlockSpec((1, w), const2),
            pl.BlockSpec((t, SSD_WIDTH), lambda b, c: (row(b, c), 2)),
            pl.BlockSpec((t, CONV_CH), lambda b, c: (row(b, c), 2)),
            pl.BlockSpec((t, LANES), lambda b, c: (row(b, c), 0)),
            pl.BlockSpec((CONV_WIDTH, CONV_CH), const2),
            pl.BlockSpec((1, CONV_CH), const2),
            pl.BlockSpec((1, LANES), const2),
            pl.BlockSpec((1, LANES), const2),
            pl.BlockSpec((1, SSD_WIDTH), const2),
            pl.BlockSpec((1, SSD_WIDTH), const2),
            pl.BlockSpec((LANES, SSD_WIDTH), const2),
        ],
        out_specs=[
            pl.BlockSpec((t, w), lambda b, c: (row(b, c), 0)),
            pl.BlockSpec((t, SSD_WIDTH), lambda b, c: (row(b, c), 0)),
        ],
        out_shape=[
            jax.ShapeDtypeStruct((batch * seq, w), BF16),
            jax.ShapeDtypeStruct((batch * seq, SSD_WIDTH), BF16),
        ],
        scratch_shapes=[
            pltpu.VMEM((RET_HEADS, RET_HEAD_DIM, RET_HEAD_DIM), F32),
            pltpu.VMEM((t + SUBLANES, CONV_CH), F32),
            pltpu.VMEM((SSD_STATE, SSD_WIDTH), F32),
            pltpu.VMEM((t, CONV_CH), F32),
        ],
        compiler_params=_cparams(("parallel", "arbitrary")),
        name="mixer",
    )(qk, qk, rest, rest, dmask, qd, kd, ret_gain, ret_bias,
      rest, rest, dtr, convw, convb, dtb, a_log, dexp, ngain, expand)


def _outproj_ln_kernel(ret_ref, ssd_ref, x_ref, wo_ref, gain_ref, bias_ref, o_ref, *, alpha):
    for r in range(OUT_TM // OUT_ROWS):
        rows = slice(r * OUT_ROWS, (r + 1) * OUT_ROWS)
        mix = _dot(ret_ref[rows, :], wo_ref[0:RET_WIDTH, :]) + _dot(ssd_ref[rows, :], wo_ref[RET_WIDTH:, :])
        y = alpha * x_ref[rows, :] + mix
        o_ref[rows, :] = _layer_norm_rows(y, gain_ref[...], bias_ref[...])


def _outproj_ln(ret, ssd, x, wo, gain, bias, alpha):
    m, d = x.shape
    return pl.pallas_call(
        functools.partial(_outproj_ln_kernel, alpha=alpha),
        grid=(m // OUT_TM,),
        in_specs=[
            pl.BlockSpec((OUT_TM, RET_WIDTH), lambda i: (i, 0)),
            pl.BlockSpec((OUT_TM, SSD_WIDTH), lambda i: (i, 0)),
            pl.BlockSpec((OUT_TM, d), lambda i: (i, 0)),
            pl.BlockSpec((RET_WIDTH + SSD_WIDTH, d), lambda i: (0, 0), pipeline_mode=pl.Buffered(1)),
            pl.BlockSpec((1, d), lambda i: (0, 0)),
            pl.BlockSpec((1, d), lambda i: (0, 0)),
        ],
        out_specs=pl.BlockSpec((OUT_TM, d), lambda i: (i, 0)),
        out_shape=jax.ShapeDtypeStruct((m, d), F32),
        compiler_params=_cparams(("parallel",)),
        name="outproj_ln",
    )(ret, ssd, x, wo, gain, bias)


def _w_in_prep_kernel(wt_ref, perm_ref, qk_ref, rest_ref, dt_ref, *, n_qk, n_rest):
    j = pl.program_id(0)

    @pl.when(j < n_qk)
    def _():
        reordered = _dot(perm_ref[...], wt_ref[...].astype(BF16))
        qk_ref[...] = reordered.T.astype(BF16)

    @pl.when(jnp.logical_and(j >= n_qk, j < n_qk + n_rest))
    def _():
        rest_ref[...] = wt_ref[...].T.astype(BF16)

    @pl.when(j == n_qk + n_rest)
    def _():
        w_dt = wt_ref[0:SSD_HEADS, :].astype(BF16)
        r = lax.broadcasted_iota(jnp.int32, (SSD_HEADS, LANES), 0)
        l = lax.broadcasted_iota(jnp.int32, (SSD_HEADS, LANES), 1)
        embed = jnp.where(r == l, 1.0, 0.0).astype(BF16)
        dt_ref[...] = lax.dot_general(w_dt, embed, (((0,), (0,)), ((), ())),
                                      preferred_element_type=F32).astype(BF16)


def _w_in_prep(w_in_layers, layer, perm_t):
    wt = jnp.swapaxes(w_in_layers, 1, 2)
    _, n, k = wt.shape
    rows = W_PREP_ROWS
    n_qk = 2 * RET_WIDTH // rows
    n_rest = (2 * RET_WIDTH + SSD_WIDTH + CONV_CH) // rows
    return pl.pallas_call(
        functools.partial(_w_in_prep_kernel, n_qk=n_qk, n_rest=n_rest),
        grid=(n_qk + n_rest + 1,),
        in_specs=[
            pl.BlockSpec((None, rows, k), lambda j: (layer, j, 0)),
            pl.BlockSpec((rows, rows), lambda j: (0, 0)),
        ],
        out_specs=[
            pl.BlockSpec((k, rows), lambda j: (0, jnp.minimum(j, n_qk - 1))),
            pl.BlockSpec((k, rows), lambda j: (0, jnp.clip(j - n_qk, 0, n_rest - 1))),
            pl.BlockSpec((k, LANES), lambda j: (0, 0)),
        ],
        out_shape=[
            jax.ShapeDtypeStruct((k, n_qk * rows), BF16),
            jax.ShapeDtypeStruct((k, n_rest * rows), BF16),
            jax.ShapeDtypeStruct((k, LANES), BF16),
        ],
        compiler_params=_cparams(("arbitrary",)),
        name="w_in_prep",
    )(wt, perm_t)


def _even_odd_permutation_t():
    half = RET_HEAD_DIM // 2
    j = jnp.arange(RET_HEAD_DIM)
    src = jnp.where(j < half, 2 * j, 2 * (j - half) + 1)
    return (src[:, None] == jnp.arange(RET_HEAD_DIM)[None, :]).astype(BF16)


def kernel(x, positions, ffn1_w_gate, ffn1_w_up, ffn1_w_down, ln1_gain, ln1_bias, mix_w_in, ret_gn_gain, ret_gn_bias, ssd_conv_w, ssd_conv_b, ssd_dt_bias, ssd_a_log, ssd_d, ssd_norm_gain, mix_w_out, ln2_gain, ln2_bias, ffn2_w_gate, ffn2_w_up, ffn2_w_down, ln3_gain, ln3_bias):
    batch, seq, d = x.shape
    depth = ffn1_w_gate.shape[0]
    alpha = (2.0 * depth) ** 0.25
    m = batch * seq
    h = x.reshape(m, d)

    half = RET_HEAD_DIM // 2
    inv_freq = (1.0 / (ROPE_BASE ** jnp.linspace(0.0, 1.0, half, dtype=F32))).reshape(1, half)
    cos, sin = _rope_tables(positions.reshape(m, 1), inv_freq)
    dmask, qd, kd = _retention_consts(RET_CHUNK)
    perm_t = _even_odd_permutation_t()
    expand = (jnp.arange(SSD_WIDTH)[None, :] // SSD_HEAD_DIM == jnp.arange(LANES)[:, None]).astype(BF16)
    row = lambda v: v.reshape(1, -1).astype(F32)
    pad_lanes = lambda v: jnp.pad(v.astype(F32), (0, LANES - v.shape[0])).reshape(1, LANES)

    for layer in range(depth):
        w_qk, w_rest, w_dt = _w_in_prep(mix_w_in, layer, perm_t)

        h, ffn2_w = _ffn_ln(h, ffn1_w_gate[layer].astype(BF16), ffn1_w_up[layer].astype(BF16),
                            ffn1_w_down[layer].astype(BF16), row(ln1_gain[layer]), row(ln1_bias[layer]), alpha,
                            later_ffn=(ffn2_w_gate, ffn2_w_up, ffn2_w_down, layer))

        qk, rest, dtr = _in_proj(h, w_qk, w_rest, w_dt, cos, sin)

        dexp = jnp.repeat(ssd_d[layer].astype(F32), SSD_HEAD_DIM).reshape(1, SSD_WIDTH)
        ret, ssd = _mixer(qk, rest, dtr, dmask, qd, kd, row(ret_gn_gain[layer]), row(ret_gn_bias[layer]),
                          ssd_conv_w[layer].astype(F32), row(ssd_conv_b[layer]), pad_lanes(ssd_dt_bias[layer]),
                          pad_lanes(ssd_a_log[layer]), dexp, row(ssd_norm_gain[layer]), expand, batch, seq)

        h = _outproj_ln(ret, ssd, h, mix_w_out[layer].astype(BF16), row(ln2_gain[layer]),
                        row(ln2_bias[layer]), alpha)

        h, _ = _ffn_ln(h, *ffn2_w, row(ln3_gain[layer]), row(ln3_bias[layer]), alpha)

    return h.reshape(batch, seq, d)
```

```python
import functools
import math

import jax
import jax.numpy as jnp
from jax import lax
from jax.experimental import pallas as pl
from jax.experimental.pallas import tpu as pltpu

F32 = jnp.float32
BF16 = jnp.bfloat16

D_MODEL = 2048
RET_WIDTH = 1024
RET_HEAD_DIM = 256
RET_HEADS = 4
SSD_WIDTH = 1024
SSD_HEAD_DIM = 64
SSD_HEADS = 16
SSD_GROUPS = 2
SSD_STATE = 128
CONV_WIDTH = 4
CONV_CH = SSD_WIDTH + 2 * SSD_GROUPS * SSD_STATE
D_FF = 5632
ROPE_BASE = 10000.0
LN_EPS = 1e-5
FFN_RES_WEIGHT = 0.5

LANES = 128
SUBLANES = 8
VMEM_BYTES_V7X = 64 * 1024 * 1024
VMEM_LIMIT = VMEM_BYTES_V7X - 4 * 1024 * 1024
VMEM_LIMIT_FFN = VMEM_BYTES_V7X - 1024 * 1024

FFN_TM = 1024
FFN_TF = 512
FFN_ROWS = 512
FFN_LN_ROWS = 256
PROJ_TM = 1024
PROJ_TN_QK = 1024
PROJ_TN_REST = 768
OUT_TM = 1024
OUT_ROWS = 128
RET_CHUNK = 256
SSD_CHUNK = 128
MIX_STEP = 512
ROPE_TM = 1024
W_PREP_ROWS = 256


def _cparams(sem, vmem_limit=VMEM_LIMIT):
    return pltpu.CompilerParams(dimension_semantics=sem, vmem_limit_bytes=vmem_limit)


def _silu(v):
    return v / (1.0 + jnp.exp(-v))


def _layer_norm_rows(y, gain, bias):
    mu = jnp.mean(y, axis=-1, keepdims=True)
    d = y - mu
    var = jnp.mean(d * d, axis=-1, keepdims=True)
    return d * lax.rsqrt(var + LN_EPS) * gain + bias


def _split3(v):
    h0 = v.astype(BF16)
    r1 = v - h0.astype(F32)
    h1 = r1.astype(BF16)
    h2 = (r1 - h1.astype(F32)).astype(BF16)
    return h0, h1, h2


def _dot(a, b):
    return jnp.dot(a, b, preferred_element_type=F32)


def _dot_exact01(m01, v, left=True):
    parts = _split3(v)
    if left:
        return _dot(m01, parts[0]) + _dot(m01, parts[1]) + _dot(m01, parts[2])
    return _dot(parts[0], m01) + _dot(parts[1], m01) + _dot(parts[2], m01)


def _ffn_ln_kernel(*refs, alpha, n_f, n_cast):
    x_ref, wg_ref, wu_ref, wd_ref, gain_ref, bias_ref = refs[:6]
    cast_in = refs[6:6 + n_cast]
    o_ref = refs[6 + n_cast]
    cast_out = refs[7 + n_cast:7 + 2 * n_cast]
    xb_ref = refs[7 + 2 * n_cast]
    f = pl.program_id(1)

    for src, dst in zip(cast_in, cast_out):
        dst[...] = src[...].astype(dst.dtype)

    def partial_out(rows):
        xb = xb_ref[rows, :]
        h = (_silu(_dot(xb, wg_ref[...])) * _dot(xb, wu_ref[...])).astype(BF16)
        return _dot(h, wd_ref[...])

    @pl.when(f == 0)
    def _():
        for r in range(FFN_TM // FFN_ROWS):
            rows = slice(r * FFN_ROWS, (r + 1) * FFN_ROWS)
            xb_ref[rows, :] = x_ref[rows, :].astype(BF16)
            o_ref[rows, :] = partial_out(rows)

    @pl.when(jnp.logical_and(f > 0, f < n_f - 1))
    def _():
        for r in range(FFN_TM // FFN_ROWS):
            rows = slice(r * FFN_ROWS, (r + 1) * FFN_ROWS)
            o_ref[rows, :] += partial_out(rows)

    @pl.when(f == n_f - 1)
    def _():
        for r in range(FFN_TM // FFN_LN_ROWS):
            rows = slice(r * FFN_LN_ROWS, (r + 1) * FFN_LN_ROWS)
            y = alpha * x_ref[rows, :] + FFN_RES_WEIGHT * (o_ref[rows, :] + partial_out(rows))
            o_ref[rows, :] = _layer_norm_rows(y, gain_ref[...], bias_ref[...])


def _ffn_ln(x, wg, wu, wd, gain, bias, alpha, later_ffn=None):
    m, d = x.shape
    n_f = D_FF // FFN_TF
    n_i = m // FFN_TM
    cast_args, cast_in_specs, cast_out_specs, cast_shapes = [], [], [], []
    if later_ffn is not None:
        lg, lu, ld, layer = later_ffn
        gu_block = (d // n_i, FFN_TF)
        dn_block = (D_FF // (n_i * n_f), d)
        gu_idx = lambda i, f: (i, f)
        dn_idx = lambda i, f: (i * n_f + f, 0)
        for w, block, idx in ((lg, gu_block, gu_idx), (lu, gu_block, gu_idx), (ld, dn_block, dn_idx)):
            cast_args.append(w)
            cast_in_specs.append(pl.BlockSpec((None,) + block, lambda i, f, idx=idx: (layer,) + idx(i, f)))
            cast_out_specs.append(pl.BlockSpec(block, idx))
            cast_shapes.append(jax.ShapeDtypeStruct(w.shape[1:], BF16))
    outs = pl.pallas_call(
        functools.partial(_ffn_ln_kernel, alpha=alpha, n_f=n_f, n_cast=len(cast_args)),
        grid=(n_i, n_f),
        in_specs=[
            pl.BlockSpec((FFN_TM, d), lambda i, f: (i, 0)),
            pl.BlockSpec((d, FFN_TF), lambda i, f: (0, f)),
            pl.BlockSpec((d, FFN_TF), lambda i, f: (0, f)),
            pl.BlockSpec((FFN_TF, d), lambda i, f: (f, 0)),
            pl.BlockSpec((1, d), lambda i, f: (0, 0)),
            pl.BlockSpec((1, d), lambda i, f: (0, 0)),
        ] + cast_in_specs,
        out_specs=[pl.BlockSpec((FFN_TM, d), lambda i, f: (i, 0))] + cast_out_specs,
        out_shape=[jax.ShapeDtypeStruct((m, d), F32)] + cast_shapes,
        scratch_shapes=[pltpu.VMEM((FFN_TM, d), BF16)],
        compiler_params=_cparams(("parallel", "arbitrary"), VMEM_LIMIT_FFN),
        name="ffn_ln",
    )(x, wg, wu, wd, gain, bias, *cast_args)
    return outs[0], tuple(outs[1:])


def _rope_tab_kernel(pos_ref, invf_ref, cos_ref, sin_ref):
    theta = pos_ref[...].astype(F32) * invf_ref[...]
    cos_ref[...] = jnp.cos(theta)
    sin_ref[...] = jnp.sin(theta)


def _rope_tables(pos_col, inv_freq):
    m = pos_col.shape[0]
    half = inv_freq.shape[1]
    return pl.pallas_call(
        _rope_tab_kernel,
        grid=(m // ROPE_TM,),
        in_specs=[
            pl.BlockSpec((ROPE_TM, 1), lambda i: (i, 0)),
            pl.BlockSpec((1, half), lambda i: (0, 0)),
        ],
        out_specs=[
            pl.BlockSpec((ROPE_TM, half), lambda i: (i, 0)),
            pl.BlockSpec((ROPE_TM, half), lambda i: (i, 0)),
        ],
        out_shape=[jax.ShapeDtypeStruct((m, half), F32)] * 2,
        compiler_params=_cparams(("parallel",)),
        name="rope_tab",
    )(pos_col, inv_freq)


def _in_proj_kernel(xa_ref, xb_half_ref, wqk_ref, wrest_ref, wdt_ref, cos_ref, sin_ref, qk_ref, rest_ref, dt_ref,
                    xb_ref, *, n_qk, k_scale):
    j = pl.program_id(1)

    @pl.when(j == 0)
    def _():
        hr = xa_ref.shape[0]
        xb_ref[0:hr, :] = xa_ref[...].astype(BF16)
        xb_ref[hr:, :] = xb_half_ref[...].astype(BF16)
        dt_ref[...] = _dot(xb_ref[...], wdt_ref[...])

    @pl.when(j < n_qk)
    def _():
        acc = _dot(xb_ref[...], wqk_ref[...])
        scale = jnp.where(j >= n_qk // 2, k_scale, 1.0).astype(F32)
        c = cos_ref[...] * scale
        s = sin_ref[...] * scale
        half = RET_HEAD_DIM // 2
        for hd in range(acc.shape[1] // RET_HEAD_DIM):
            lo = hd * RET_HEAD_DIM
            e = acc[:, lo:lo + half]
            o = acc[:, lo + half:lo + RET_HEAD_DIM]
            qk_ref[:, lo:lo + half] = (e * c - o * s).astype(qk_ref.dtype)
            qk_ref[:, lo + half:lo + RET_HEAD_DIM] = (o * c + e * s).astype(qk_ref.dtype)

    @pl.when(j >= n_qk)
    def _():
        rest_ref[...] = _dot(xb_ref[...], wrest_ref[...]).astype(rest_ref.dtype)


def _in_proj(x, w_qk, w_rest, w_dt, cos, sin):
    m, k = x.shape
    half = RET_HEAD_DIM // 2
    tn_qk, tn_rest = PROJ_TN_QK, PROJ_TN_REST
    n_qk = w_qk.shape[1] // tn_qk
    n_rest = w_rest.shape[1] // tn_rest
    qk_j = lambda j: jnp.minimum(j, n_qk - 1)
    rest_j = lambda j: jnp.maximum(j - n_qk, 0)
    n_i = m // PROJ_TM
    n_j = n_qk + n_rest
    half_rows = lambda h, switch: (lambda i, j: (2 * jnp.minimum(i + (j >= switch), n_i - 1) + h, 0))
    return pl.pallas_call(
        functools.partial(_in_proj_kernel, n_qk=n_qk, k_scale=RET_HEAD_DIM ** -0.5),
        grid=(n_i, n_j),
        in_specs=[
            pl.BlockSpec((PROJ_TM // 2, k), half_rows(0, n_j // 3)),
            pl.BlockSpec((PROJ_TM // 2, k), half_rows(1, (2 * n_j) // 3)),
            pl.BlockSpec((k, tn_qk), lambda i, j: (0, qk_j(j))),
            pl.BlockSpec((k, tn_rest), lambda i, j: (0, rest_j(j))),
            pl.BlockSpec((k, LANES), lambda i, j: (0, 0)),
            pl.BlockSpec((PROJ_TM, half), lambda i, j: (i, 0)),
            pl.BlockSpec((PROJ_TM, half), lambda i, j: (i, 0)),
        ],
        out_specs=[
            pl.BlockSpec((PROJ_TM, tn_qk), lambda i, j: (i, qk_j(j))),
            pl.BlockSpec((PROJ_TM, tn_rest), lambda i, j: (i, rest_j(j))),
            pl.BlockSpec((PROJ_TM, LANES), lambda i, j: (i, 0)),
        ],
        out_shape=[
            jax.ShapeDtypeStruct((m, n_qk * tn_qk), BF16),
            jax.ShapeDtypeStruct((m, n_rest * tn_rest), BF16),
            jax.ShapeDtypeStruct((m, LANES), F32),
        ],
        scratch_shapes=[pltpu.VMEM((PROJ_TM, k), BF16)],
        compiler_params=_cparams(("parallel", "arbitrary")),
        name="in_proj",
    )(x, x, w_qk, w_rest, w_dt, cos, sin)


def _ret_log_gamma(h):
    return math.log(1.0 - 2.0 ** (-5.0 - h))


def _retention_chunk(rows, q_ref, k_ref, v_ref, g_ref, dmask_ref, qd_ref, kd_ref, gain_ref, bias_ref,
                     o_ref, state_ref):
    tc = rows.stop - rows.start
    heads = range(RET_HEADS)
    sl = [slice(h * RET_HEAD_DIM, (h + 1) * RET_HEAD_DIM) for h in heads]
    scores = [lax.dot_general(q_ref[rows, sl[h]], k_ref[rows, sl[h]], (((1,), (1,)), ((), ())),
                              preferred_element_type=F32) for h in heads]
    inter = [_dot(q_ref[rows, sl[h]], state_ref[h].astype(BF16)) for h in heads]
    kv = []
    for h in heads:
        vkd = (v_ref[rows, sl[h]].astype(F32) * kd_ref[:, sl[h]]).astype(BF16)
        kv.append(lax.dot_general(k_ref[rows, sl[h]], vkd, (((0,), (0,)), ((), ())), preferred_element_type=F32))
    probs = [(scores[h] * dmask_ref[h]).astype(BF16) for h in heads]
    outs = [_dot(probs[h], v_ref[rows, sl[h]]) + inter[h] * qd_ref[:, sl[h]] for h in heads]
    for h in heads:
        state_ref[h] = state_ref[h] * math.exp(_ret_log_gamma(h) * tc) + kv[h]
    for h in heads:
        o = outs[h]
        mu = jnp.mean(o, axis=-1, keepdims=True)
        d = o - mu
        var = jnp.mean(d * d, axis=-1, keepdims=True)
        on = d * lax.rsqrt(var + LN_EPS) * gain_ref[:, sl[h]] + bias_ref[:, sl[h]]
        o_ref[rows, sl[h]] = (_silu(g_ref[rows, sl[h]].astype(F32)) * on).astype(o_ref.dtype)


def _retention_consts(tc):
    lg = jnp.asarray([_ret_log_gamma(h) for h in range(RET_HEADS)], F32)
    pos = jnp.arange(tc, dtype=F32)
    rel = pos[:, None] - pos[None, :]
    dmask = jnp.where(rel >= 0, jnp.exp(lg[:, None, None] * jnp.maximum(rel, 0.0)), 0.0)
    qd = jnp.exp(lg[None, :] * (pos[:, None] + 1.0))
    kd = jnp.exp(lg[None, :] * (tc - 1.0 - pos[:, None]))
    rep = lambda t: jnp.repeat(t, RET_HEAD_DIM, axis=1)
    return dmask.astype(F32), rep(qd).astype(F32), rep(kd).astype(F32)


def _ssd_conv(xbc_ref, convw_ref, convb_ref, xpad_ref, xc_ref):
    t = xbc_ref.shape[0]
    xf = xbc_ref[...].astype(F32)
    xpad_ref[SUBLANES:, :] = xf
    conv = convb_ref[...] + convw_ref[CONV_WIDTH - 1:CONV_WIDTH, :] * xf
    for s in range(1, CONV_WIDTH):
        w_s = convw_ref[CONV_WIDTH - 1 - s:CONV_WIDTH - s, :]
        conv = conv + w_s * xpad_ref[SUBLANES - s:SUBLANES - s + t, :]
    xc_ref[...] = _silu(conv)
    xpad_ref[0:SUBLANES, :] = xf[t - SUBLANES:t, :]


def _ssd_chunks(chunks, z_ref, dtr_ref, dtb_ref, alog_ref, dexp_ref, ngain_ref, expand_ref, o_ref, state_ref,
                xc_ref):
    ts = SSD_CHUNK
    n = SSD_STATE
    gw = SSD_WIDTH // SSD_GROUPS
    ri = lax.broadcasted_iota(jnp.int32, (ts, ts), 0)
    ci = lax.broadcasted_iota(jnp.int32, (ts, ts), 1)
    causal = ri >= ci
    tri = jnp.where(causal, 1.0, 0.0).astype(BF16)
    lane_lo = ci < SSD_HEAD_DIM
    expand = expand_ref[...]
    a_neg = -jnp.exp(alog_ref[...])

    for j in chunks:
        rows = slice(j * ts, (j + 1) * ts)
        xs = xc_ref[rows, 0:SSD_WIDTH]
        bm = xc_ref[rows, SSD_WIDTH:SSD_WIDTH + SSD_GROUPS * n].astype(BF16)
        cm = xc_ref[rows, SSD_WIDTH + SSD_GROUPS * n:CONV_CH].astype(BF16)

        pre = dtr_ref[rows, :] + dtb_ref[...]
        dt = jnp.maximum(pre, 0.0) + jnp.log1p(jnp.exp(-jnp.abs(pre)))
        da = dt * a_neg
        acum = _dot_exact01(tri, da, left=True)
        acum_t = acum.T
        acum_e = _dot_exact01(expand, acum, left=False)
        dt_e = _dot_exact01(expand, dt, left=False)
        last_e = acum_e[ts - 1:ts, :]
        from_start = jnp.exp(acum_e)
        to_end = jnp.exp(last_e - acum_e)
        total = jnp.exp(last_e)

        xdt = xs * dt_e
        xdt_te = (xdt * to_end).astype(BF16)

        zf = z_ref[rows, :].astype(F32)
        gate = _silu(zf)
        for g in range(SSD_GROUPS):
            cg = cm[:, g * n:(g + 1) * n]
            bg = bm[:, g * n:(g + 1) * n]
            cb = lax.dot_general(cg, bg, (((1,), (1,)), ((), ())), preferred_element_type=F32)
            intra = []
            for pr in range(gw // LANES):
                lanes = slice(g * gw + pr * LANES, g * gw + (pr + 1) * LANES)
                xpair = xdt[:, lanes]
                y_pair = None
                for half in range(2):
                    hd = (g * gw + pr * LANES) // SSD_HEAD_DIM + half
                    seg = acum[:, hd:hd + 1] - acum_t[hd:hd + 1, :]
                    decay = jnp.exp(jnp.where(causal, seg, -1e30))
                    wmat = (cb * decay).astype(BF16)
                    keep = lane_lo if half == 0 else jnp.logical_not(lane_lo)
                    xh = jnp.where(keep, xpair, 0.0).astype(BF16)
                    term = _dot(wmat, xh)
                    y_pair = term if y_pair is None else y_pair + term
                intra.append(y_pair)
            y_intra = jnp.concatenate(intra, axis=1)
            gl = slice(g * gw, (g + 1) * gw)
            st = state_ref[:, gl]
            y_inter = _dot(cg, st.astype(BF16)) * from_start[:, gl]
            upd = lax.dot_general(bg, xdt_te[:, gl], (((0,), (0,)), ((), ())), preferred_element_type=F32)
            state_ref[:, gl] = st * total[:, gl] + upd
            y = (y_intra + y_inter + xs[:, gl] * dexp_ref[:, gl]) * gate[:, gl]
            ms = jnp.mean(y * y, axis=-1, keepdims=True)
            o_ref[rows, gl] = (y * lax.rsqrt(ms + LN_EPS) * ngain_ref[:, gl]).astype(o_ref.dtype)


N_RET_IN = 9
N_SSD_IN = 10


def _mixer_kernel(*refs):
    ret_in = refs[:N_RET_IN]
    ssd_in = refs[N_RET_IN:N_RET_IN + N_SSD_IN]
    ret_o, ssd_o, ret_state, xpad_ref, ssd_state, xc_ref = refs[N_RET_IN + N_SSD_IN:]

    @pl.when(pl.program_id(1) == 0)
    def _():
        ret_state[...] = jnp.zeros_like(ret_state)
        xpad_ref[0:SUBLANES, :] = jnp.zeros((SUBLANES, CONV_CH), F32)
        ssd_state[...] = jnp.zeros_like(ssd_state)

    z_ref, xbc_ref, dtr_ref, convw_ref, convb_ref, dtb_ref, alog_ref, dexp_ref, ngain_ref, expand_ref = ssd_in
    for c in range(xbc_ref.shape[0] // RET_CHUNK):
        _retention_chunk(slice(c * RET_CHUNK, (c + 1) * RET_CHUNK), *ret_in, ret_o, ret_state)
    _ssd_conv(xbc_ref, convw_ref, convb_ref, xpad_ref, xc_ref)
    _ssd_chunks(range(xbc_ref.shape[0] // SSD_CHUNK), z_ref, dtr_ref, dtb_ref, alog_ref, dexp_ref, ngain_ref,
                expand_ref, ssd_o, ssd_state, xc_ref)


def _mixer(qk, rest, dtr, dmask, qd, kd, ret_gain, ret_bias, convw, convb, dtb, a_log, dexp, ngain, expand,
           batch, seq):
    t = MIX_STEP
    nc = seq // t
    w = RET_WIDTH
    row = lambda b, c: b * nc + c
    const2 = lambda b, c: (0, 0)
    return pl.pallas_call(
        _mixer_kernel,
        grid=(batch, nc),
        in_specs=[
            pl.BlockSpec((t, w), lambda b, c: (row(b, c), 0)),
            pl.BlockSpec((t, w), lambda b, c: (row(b, c), 1)),
            pl.BlockSpec((t, w), lambda b, c: (row(b, c), 0)),
            pl.BlockSpec((t, w), lambda b, c: (row(b, c), 1)),
            pl.BlockSpec((RET_HEADS, RET_CHUNK, RET_CHUNK), lambda b, c: (0, 0, 0)),
            pl.BlockSpec((RET_CHUNK, w), const2),
            pl.BlockSpec((RET_CHUNK, w), const2),
            pl.BlockSpec((1, w), const2),
            pl.BlockSpec((1, w), const2),
            pl.BlockSpec((t, SSD_WIDTH), lambda b, c: (row(b, c), 2)),
            pl.BlockSpec((t, CONV_CH), lambda b, c: (row(b, c), 2)),
            pl.BlockSpec((t, LANES), lambda b, c: (row(b, c), 0)),
            pl.BlockSpec((CONV_WIDTH, CONV_CH), const2),
            pl.BlockSpec((1, CONV_CH), const2),
            pl.BlockSpec((1, LANES), const2),
            pl.BlockSpec((1, LANES), const2),
            pl.BlockSpec((1, SSD_WIDTH), const2),
            pl.BlockSpec((1, SSD_WIDTH), const2),
            pl.BlockSpec((LANES, SSD_WIDTH), const2),
        ],
        out_specs=[
            pl.BlockSpec((t, w), lambda b, c: (row(b, c), 0)),
            pl.BlockSpec((t, SSD_WIDTH), lambda b, c: (row(b, c), 0)),
        ],
        out_shape=[
            jax.ShapeDtypeStruct((batch * seq, w), BF16),
            jax.ShapeDtypeStruct((batch * seq, SSD_WIDTH), BF16),
        ],
        scratch_shapes=[
            pltpu.VMEM((RET_HEADS, RET_HEAD_DIM, RET_HEAD_DIM), F32),
            pltpu.VMEM((t + SUBLANES, CONV_CH), F32),
            pltpu.VMEM((SSD_STATE, SSD_WIDTH), F32),
            pltpu.VMEM((t, CONV_CH), F32),
        ],
        compiler_params=_cparams(("parallel", "arbitrary")),
        name="mixer",
    )(qk, qk, rest, rest, dmask, qd, kd, ret_gain, ret_bias,
      rest, rest, dtr, convw, convb, dtb, a_log, dexp, ngain, expand)


def _outproj_ln_kernel(ret_ref, ssd_ref, x_ref, wo_ref, gain_ref, bias_ref, o_ref, *, alpha):
    for r in range(OUT_TM // OUT_ROWS):
        rows = slice(r * OUT_ROWS, (r + 1) * OUT_ROWS)
        mix = _dot(ret_ref[rows, :], wo_ref[0:RET_WIDTH, :]) + _dot(ssd_ref[rows, :], wo_ref[RET_WIDTH:, :])
        y = alpha * x_ref[rows, :] + mix
        o_ref[rows, :] = _layer_norm_rows(y, gain_ref[...], bias_ref[...])


def _outproj_ln(ret, ssd, x, wo, gain, bias, alpha):
    m, d = x.shape
    return pl.pallas_call(
        functools.partial(_outproj_ln_kernel, alpha=alpha),
        grid=(m // OUT_TM,),
        in_specs=[
            pl.BlockSpec((OUT_TM, RET_WIDTH), lambda i: (i, 0)),
            pl.BlockSpec((OUT_TM, SSD_WIDTH), lambda i: (i, 0)),
            pl.BlockSpec((OUT_TM, d), lambda i: (i, 0)),
            pl.BlockSpec((RET_WIDTH + SSD_WIDTH, d), lambda i: (0, 0), pipeline_mode=pl.Buffered(1)),
            pl.BlockSpec((1, d), lambda i: (0, 0)),
            pl.BlockSpec((1, d), lambda i: (0, 0)),
        ],
        out_specs=pl.BlockSpec((OUT_TM, d), lambda i: (i, 0)),
        out_shape=jax.ShapeDtypeStruct((m, d), F32),
        compiler_params=_cparams(("parallel",)),
        name="outproj_ln",
    )(ret, ssd, x, wo, gain, bias)


def _w_in_prep_kernel(wt_ref, perm_ref, qk_ref, rest_ref, dt_ref, *, n_qk, n_rest):
    j = pl.program_id(0)

    @pl.when(j < n_qk)
    def _():
        reordered = _dot(perm_ref[...], wt_ref[...].astype(BF16))
        qk_ref[...] = reordered.T.astype(BF16)

    @pl.when(jnp.logical_and(j >= n_qk, j < n_qk + n_rest))
    def _():
        rest_ref[...] = wt_ref[...].T.astype(BF16)

    @pl.when(j == n_qk + n_rest)
    def _():
        w_dt = wt_ref[0:SSD_HEADS, :].astype(BF16)
        r = lax.broadcasted_iota(jnp.int32, (SSD_HEADS, LANES), 0)
        l = lax.broadcasted_iota(jnp.int32, (SSD_HEADS, LANES), 1)
        embed = jnp.where(r == l, 1.0, 0.0).astype(BF16)
        dt_ref[...] = lax.dot_general(w_dt, embed, (((0,), (0,)), ((), ())),
                                      preferred_element_type=F32).astype(BF16)


def _w_in_prep(w_in_layers, layer, perm_t):
    wt = jnp.swapaxes(w_in_layers, 1, 2)
    _, n, k = wt.shape
    rows = W_PREP_ROWS
    n_qk = 2 * RET_WIDTH // rows
    n_rest = (2 * RET_WIDTH + SSD_WIDTH + CONV_CH) // rows
    return pl.pallas_call(
        functools.partial(_w_in_prep_kernel, n_qk=n_qk, n_rest=n_rest),
        grid=(n_qk + n_rest + 1,),
        in_specs=[
            pl.BlockSpec((None, rows, k), lambda j: (layer, j, 0)),
            pl.BlockSpec((rows, rows), lambda j: (0, 0)),
        ],
        out_specs=[
            pl.BlockSpec((k, rows), lambda j: (0, jnp.minimum(j, n_qk - 1))),
            pl.BlockSpec((k, rows), lambda j: (0, jnp.clip(j - n_qk, 0, n_rest - 1))),
            pl.BlockSpec((k, LANES), lambda j: (0, 0)),
        ],
        out_shape=[
            jax.ShapeDtypeStruct((k, n_qk * rows), BF16),
            jax.ShapeDtypeStruct((k, n_rest * rows), BF16),
            jax.ShapeDtypeStruct((k, LANES), BF16),
        ],
        compiler_params=_cparams(("arbitrary",)),
        name="w_in_prep",
    )(wt, perm_t)


def _even_odd_permutation_t():
    half = RET_HEAD_DIM // 2
    j = jnp.arange(RET_HEAD_DIM)
    src = jnp.where(j < half, 2 * j, 2 * (j - half) + 1)
    return (src[:, None] == jnp.arange(RET_HEAD_DIM)[None, :]).astype(BF16)


def kernel(x, positions, ffn1_w_gate, ffn1_w_up, ffn1_w_down, ln1_gain, ln1_bias, mix_w_in, ret_gn_gain, ret_gn_bias, ssd_conv_w, ssd_conv_b, ssd_dt_bias, ssd_a_log, ssd_d, ssd_norm_gain, mix_w_out, ln2_gain, ln2_bias, ffn2_w_gate, ffn2_w_up, ffn2_w_down, ln3_gain, ln3_bias):
    batch, seq, d = x.shape
    depth = ffn1_w_gate.shape[0]
    alpha = (2.0 * depth) ** 0.25
    m = batch * seq
    h = x.reshape(m, d)

    half = RET_HEAD_DIM // 2
    inv_freq = (1.0 / (ROPE_BASE ** jnp.linspace(0.0, 1.0, half, dtype=F32))).reshape(1, half)
    cos, sin = _rope_tables(positions.reshape(m, 1), inv_freq)
    dmask, qd, kd = _retention_consts(RET_CHUNK)
    perm_t = _even_odd_permutation_t()
    expand = (jnp.arange(SSD_WIDTH)[None, :] // SSD_HEAD_DIM == jnp.arange(LANES)[:, None]).astype(BF16)
    row = lambda v: v.reshape(1, -1).astype(F32)
    pad_lanes = lambda v: jnp.pad(v.astype(F32), (0, LANES - v.shape[0])).reshape(1, LANES)

    for layer in range(depth):
        w_qk, w_rest, w_dt = _w_in_prep(mix_w_in, layer, perm_t)

        h, ffn2_w = _ffn_ln(h, ffn1_w_gate[layer].astype(BF16), ffn1_w_up[layer].astype(BF16),
                            ffn1_w_down[layer].astype(BF16), row(ln1_gain[layer]), row(ln1_bias[layer]), alpha,
                            later_ffn=(ffn2_w_gate, ffn2_w_up, ffn2_w_down, layer))

        qk, rest, dtr = _in_proj(h, w_qk, w_rest, w_dt, cos, sin)

        dexp = jnp.repeat(ssd_d[layer].astype(F32), SSD_HEAD_DIM).reshape(1, SSD_WIDTH)
        ret, ssd = _mixer(qk, rest, dtr, dmask, qd, kd, row(ret_gn_gain[layer]), row(ret_gn_bias[layer]),
                          ssd_conv_w[layer].astype(F32), row(ssd_conv_b[layer]), pad_lanes(ssd_dt_bias[layer]),
                          pad_lanes(ssd_a_log[layer]), dexp, row(ssd_norm_gain[layer]), expand, batch, seq)

        h = _outproj_ln(ret, ssd, h, mix_w_out[layer].astype(BF16), row(ln2_gain[layer]),
                        row(ln2_bias[layer]), alpha)

        h, _ = _ffn_ln(h, *ffn2_w, row(ln3_gain[layer]), row(ln3_bias[layer]), alpha)

    return h.reshape(batch, seq, d)
```

```python
import functools
import math

import jax
import jax.numpy as jnp
from jax import lax
from jax.experimental import pallas as pl
from jax.experimental.pallas import tpu as pltpu

F32 = jnp.float32
BF16 = jnp.bfloat16

D_MODEL = 2048
RET_WIDTH = 1024
RET_HEAD_DIM = 256
RET_HEADS = 4
SSD_WIDTH = 1024
SSD_HEAD_DIM = 64
SSD_HEADS = 16
SSD_GROUPS = 2
SSD_STATE = 128
CONV_WIDTH = 4
CONV_CH = SSD_WIDTH + 2 * SSD_GROUPS * SSD_STATE
D_FF = 5632
ROPE_BASE = 10000.0
LN_EPS = 1e-5
FFN_RES_WEIGHT = 0.5

LANES = 128
SUBLANES = 8
VMEM_BYTES_V7X = 64 * 1024 * 1024
VMEM_LIMIT = VMEM_BYTES_V7X - 4 * 1024 * 1024
VMEM_LIMIT_FFN = VMEM_BYTES_V7X - 1024 * 1024

FFN_TM = 1024
FFN_TF = 512
FFN_ROWS = 512
FFN_LN_ROWS = 256
PROJ_TM = 1024
PROJ_TN_QK = 1024
PROJ_TN_REST = 768
OUT_TM = 1024
OUT_ROWS = 128
RET_CHUNK = 256
SSD_CHUNK = 128
MIX_STEP = 512
ROPE_TM = 1024
W_PREP_ROWS = 256


def _cparams(sem, vmem_limit=VMEM_LIMIT):
    return pltpu.CompilerParams(dimension_semantics=sem, vmem_limit_bytes=vmem_limit)


def _silu(v):
    h = 0.5 * v
    return h + h * jnp.tanh(h)


def _layer_norm_rows(y, gain, bias):
    mu = jnp.mean(y, axis=-1, keepdims=True)
    d = y - mu
    var = jnp.mean(d * d, axis=-1, keepdims=True)
    return d * lax.rsqrt(var + LN_EPS) * gain + bias


def _split3(v):
    h0 = v.astype(BF16)
    r1 = v - h0.astype(F32)
    h1 = r1.astype(BF16)
    h2 = (r1 - h1.astype(F32)).astype(BF16)
    return h0, h1, h2


def _dot(a, b):
    return jnp.dot(a, b, preferred_element_type=F32)


def _dot_exact01(m01, v, left=True):
    parts = _split3(v)
    if left:
        return _dot(m01, parts[0]) + _dot(m01, parts[1]) + _dot(m01, parts[2])
    return _dot(parts[0], m01) + _dot(parts[1], m01) + _dot(parts[2], m01)


def _ffn_ln_kernel(*refs, alpha, n_f, n_cast):
    x_ref, wg_ref, wu_ref, wd_ref, gain_ref, bias_ref = refs[:6]
    cast_in = refs[6:6 + n_cast]
    o_ref = refs[6 + n_cast]
    cast_out = refs[7 + n_cast:7 + 2 * n_cast]
    xb_ref = refs[7 + 2 * n_cast]
    f = pl.program_id(1)

    for src, dst in zip(cast_in, cast_out):
        dst[...] = src[...].astype(dst.dtype)

    def partial_out(rows):
        xb = xb_ref[rows, :]
        h = (_silu(_dot(xb, wg_ref[...])) * _dot(xb, wu_ref[...])).astype(BF16)
        return _dot(h, wd_ref[...])

    @pl.when(f == 0)
    def _():
        for r in range(FFN_TM // FFN_ROWS):
            rows = slice(r * FFN_ROWS, (r + 1) * FFN_ROWS)
            xb_ref[rows, :] = x_ref[rows, :].astype(BF16)
            o_ref[rows, :] = partial_out(rows)

    @pl.when(jnp.logical_and(f > 0, f < n_f - 1))
    def _():
        for r in range(FFN_TM // FFN_ROWS):
            rows = slice(r * FFN_ROWS, (r + 1) * FFN_ROWS)
            o_ref[rows, :] += partial_out(rows)

    @pl.when(f == n_f - 1)
    def _():
        for r in range(FFN_TM // FFN_LN_ROWS):
            rows = slice(r * FFN_LN_ROWS, (r + 1) * FFN_LN_ROWS)
            y = alpha * x_ref[rows, :] + FFN_RES_WEIGHT * (o_ref[rows, :] + partial_out(rows))
            o_ref[rows, :] = _layer_norm_rows(y, gain_ref[...], bias_ref[...])


def _ffn_ln(x, wg, wu, wd, gain, bias, alpha, later_ffn=None):
    m, d = x.shape
    n_f = D_FF // FFN_TF
    n_i = m // FFN_TM
    cast_args, cast_in_specs, cast_out_specs, cast_shapes = [], [], [], []
    if later_ffn is not None:
        lg, lu, ld, layer = later_ffn
        gu_block = (d // n_i, FFN_TF)
        dn_block = (D_FF // (n_i * n_f), d)
        gu_idx = lambda i, f: (i, f)
        dn_idx = lambda i, f: (i * n_f + f, 0)
        for w, block, idx in ((lg, gu_block, gu_idx), (lu, gu_block, gu_idx), (ld, dn_block, dn_idx)):
            cast_args.append(w)
            cast_in_specs.append(pl.BlockSpec((None,) + block, lambda i, f, idx=idx: (layer,) + idx(i, f)))
            cast_out_specs.append(pl.BlockSpec(block, idx))
            cast_shapes.append(jax.ShapeDtypeStruct(w.shape[1:], BF16))
    outs = pl.pallas_call(
        functools.partial(_ffn_ln_kernel, alpha=alpha, n_f=n_f, n_cast=len(cast_args)),
        grid=(n_i, n_f),
        in_specs=[
            pl.BlockSpec((FFN_TM, d), lambda i, f: (i, 0)),
            pl.BlockSpec((d, FFN_TF), lambda i, f: (0, f)),
            pl.BlockSpec((d, FFN_TF), lambda i, f: (0, f)),
            pl.BlockSpec((FFN_TF, d), lambda i, f: (f, 0)),
            pl.BlockSpec((1, d), lambda i, f: (0, 0)),
            pl.BlockSpec((1, d), lambda i, f: (0, 0)),
        ] + cast_in_specs,
        out_specs=[pl.BlockSpec((FFN_TM, d), lambda i, f: (i, 0))] + cast_out_specs,
        out_shape=[jax.ShapeDtypeStruct((m, d), F32)] + cast_shapes,
        scratch_shapes=[pltpu.VMEM((FFN_TM, d), BF16)],
        compiler_params=_cparams(("parallel", "arbitrary"), VMEM_LIMIT_FFN),
        name="ffn_ln",
    )(x, wg, wu, wd, gain, bias, *cast_args)
    return outs[0], tuple(outs[1:])


def _rope_tab_kernel(pos_ref, invf_ref, cos_ref, sin_ref):
    theta = pos_ref[...].astype(F32) * invf_ref[...]
    cos_ref[...] = jnp.cos(theta)
    sin_ref[...] = jnp.sin(theta)


def _rope_tables(pos_col, inv_freq):
    m = pos_col.shape[0]
    half = inv_freq.shape[1]
    return pl.pallas_call(
        _rope_tab_kernel,
        grid=(m // ROPE_TM,),
        in_specs=[
            pl.BlockSpec((ROPE_TM, 1), lambda i: (i, 0)),
            pl.BlockSpec((1, half), lambda i: (0, 0)),
        ],
        out_specs=[
            pl.BlockSpec((ROPE_TM, half), lambda i: (i, 0)),
            pl.BlockSpec((ROPE_TM, half), lambda i: (i, 0)),
        ],
        out_shape=[jax.ShapeDtypeStruct((m, half), F32)] * 2,
        compiler_params=_cparams(("parallel",)),
        name="rope_tab",
    )(pos_col, inv_freq)


def _in_proj_kernel(xa_ref, xb_half_ref, wqk_ref, wrest_ref, wdt_ref, cos_ref, sin_ref, qk_ref, rest_ref, dt_ref,
                    xb_ref, *, n_qk, k_scale):
    j = pl.program_id(1)

    @pl.when(j == 0)
    def _():
        hr = xa_ref.shape[0]
        xb_ref[0:hr, :] = xa_ref[...].astype(BF16)
        xb_ref[hr:, :] = xb_half_ref[...].astype(BF16)
        dt_ref[...] = _dot(xb_ref[...], wdt_ref[...])

    @pl.when(j < n_qk)
    def _():
        acc = _dot(xb_ref[...], wqk_ref[...])
        scale = jnp.where(j >= n_qk // 2, k_scale, 1.0).astype(F32)
        c = cos_ref[...] * scale
        s = sin_ref[...] * scale
        half = RET_HEAD_DIM // 2
        for hd in range(acc.shape[1] // RET_HEAD_DIM):
            lo = hd * RET_HEAD_DIM
            e = acc[:, lo:lo + half]
            o = acc[:, lo + half:lo + RET_HEAD_DIM]
            qk_ref[:, lo:lo + half] = (e * c - o * s).astype(qk_ref.dtype)
            qk_ref[:, lo + half:lo + RET_HEAD_DIM] = (o * c + e * s).astype(qk_ref.dtype)

    @pl.when(j >= n_qk)
    def _():
        rest_ref[...] = _dot(xb_ref[...], wrest_ref[...]).astype(rest_ref.dtype)


def _in_proj(x, w_qk, w_rest, w_dt, cos, sin):
    m, k = x.shape
    half = RET_HEAD_DIM // 2
    tn_qk, tn_rest = PROJ_TN_QK, PROJ_TN_REST
    n_qk = w_qk.shape[1] // tn_qk
    n_rest = w_rest.shape[1] // tn_rest
    qk_j = lambda j: jnp.minimum(j, n_qk - 1)
    rest_j = lambda j: jnp.maximum(j - n_qk, 0)
    n_i = m // PROJ_TM
    n_j = n_qk + n_rest
    half_rows = lambda h, switch: (lambda i, j: (2 * jnp.minimum(i + (j >= switch), n_i - 1) + h, 0))
    return pl.pallas_call(
        functools.partial(_in_proj_kernel, n_qk=n_qk, k_scale=RET_HEAD_DIM ** -0.5),
        grid=(n_i, n_j),
        in_specs=[
            pl.BlockSpec((PROJ_TM // 2, k), half_rows(0, n_j // 3)),
            pl.BlockSpec((PROJ_TM // 2, k), half_rows(1, (2 * n_j) // 3)),
            pl.BlockSpec((k, tn_qk), lambda i, j: (0, qk_j(j))),
            pl.BlockSpec((k, tn_rest), lambda i, j: (0, rest_j(j))),
            pl.BlockSpec((k, LANES), lambda i, j: (0, 0)),
            pl.BlockSpec((PROJ_TM, half), lambda i, j: (i, 0)),
            pl.BlockSpec((PROJ_TM, half), lambda i, j: (i, 0)),
        ],
        out_specs=[
            pl.BlockSpec((PROJ_TM, tn_qk), lambda i, j: (i, qk_j(j))),
            pl.BlockSpec((PROJ_TM, tn_rest), lambda i, j: (i, rest_j(j))),
            pl.BlockSpec((PROJ_TM, LANES), lambda i, j: (i, 0)),
        ],
        out_shape=[
            jax.ShapeDtypeStruct((m, n_qk * tn_qk), BF16),
            jax.ShapeDtypeStruct((m, n_rest * tn_rest), BF16),
            jax.ShapeDtypeStruct((m, LANES), F32),
        ],
        scratch_shapes=[pltpu.VMEM((PROJ_TM, k), BF16)],
        compiler_params=_cparams(("parallel", "arbitrary")),
        name="in_proj",
    )(x, x, w_qk, w_rest, w_dt, cos, sin)


def _ret_log_gamma(h):
    return math.log(1.0 - 2.0 ** (-5.0 - h))


def _retention_chunk(rows, q_ref, k_ref, v_ref, g_ref, dmask_ref, qd_ref, kd_ref, gain_ref, bias_ref,
                     o_ref, state_ref):
    tc = rows.stop - rows.start
    heads = range(RET_HEADS)
    sl = [slice(h * RET_HEAD_DIM, (h + 1) * RET_HEAD_DIM) for h in heads]
    scores = [lax.dot_general(q_ref[rows, sl[h]], k_ref[rows, sl[h]], (((1,), (1,)), ((), ())),
                              preferred_element_type=F32) for h in heads]
    inter = [_dot(q_ref[rows, sl[h]], state_ref[h].astype(BF16)) for h in heads]
    kv = []
    for h in heads:
        vkd = v_ref[rows, sl[h]] * kd_ref[:, sl[h]]
        kv.append(lax.dot_general(k_ref[rows, sl[h]], vkd, (((0,), (0,)), ((), ())), preferred_element_type=F32))
    probs = [(scores[h] * dmask_ref[h]).astype(BF16) for h in heads]
    outs = [_dot(probs[h], v_ref[rows, sl[h]]) + inter[h] * qd_ref[:, sl[h]] for h in heads]
    for h in heads:
        state_ref[h] = state_ref[h] * math.exp(_ret_log_gamma(h) * tc) + kv[h]
    for h in heads:
        o = outs[h]
        mu = jnp.mean(o, axis=-1, keepdims=True)
        d = o - mu
        var = jnp.mean(d * d, axis=-1, keepdims=True)
        on = d * lax.rsqrt(var + LN_EPS) * gain_ref[:, sl[h]] + bias_ref[:, sl[h]]
        o_ref[rows, sl[h]] = (_silu(g_ref[rows, sl[h]].astype(F32)) * on).astype(o_ref.dtype)


def _retention_consts(tc):
    lg = jnp.asarray([_ret_log_gamma(h) for h in range(RET_HEADS)], F32)
    pos = jnp.arange(tc, dtype=F32)
    rel = pos[:, None] - pos[None, :]
    dmask = jnp.where(rel >= 0, jnp.exp(lg[:, None, None] * jnp.maximum(rel, 0.0)), 0.0)
    qd = jnp.exp(lg[None, :] * (pos[:, None] + 1.0))
    kd = jnp.exp(lg[None, :] * (tc - 1.0 - pos[:, None]))
    rep = lambda t: jnp.repeat(t, RET_HEAD_DIM, axis=1)
    return dmask.astype(F32), rep(qd).astype(F32), rep(kd).astype(BF16)


def _ssd_conv(xbc_ref, convw_ref, convb_ref, xpad_ref, xc_ref):
    t = xbc_ref.shape[0]
    xf = xbc_ref[...].astype(F32)
    xpad_ref[SUBLANES:, :] = xf
    conv = convb_ref[...] + convw_ref[CONV_WIDTH - 1:CONV_WIDTH, :] * xf
    for s in range(1, CONV_WIDTH):
        w_s = convw_ref[CONV_WIDTH - 1 - s:CONV_WIDTH - s, :]
        conv = conv + w_s * xpad_ref[SUBLANES - s:SUBLANES - s + t, :]
    xc_ref[...] = _silu(conv)
    xpad_ref[0:SUBLANES, :] = xf[t - SUBLANES:t, :]


def _ssd_chunks(chunks, z_ref, dtr_ref, dtb_ref, alog_ref, dexp_ref, ngain_ref, expand_ref, o_ref, state_ref,
                xc_ref):
    ts = SSD_CHUNK
    n = SSD_STATE
    gw = SSD_WIDTH // SSD_GROUPS
    ri = lax.broadcasted_iota(jnp.int32, (ts, ts), 0)
    ci = lax.broadcasted_iota(jnp.int32, (ts, ts), 1)
    causal = ri >= ci
    tri = jnp.where(causal, 1.0, 0.0).astype(BF16)
    lane_lo = ci < SSD_HEAD_DIM
    expand = expand_ref[...]
    a_neg = -jnp.exp(alog_ref[...])

    for j in chunks:
        rows = slice(j * ts, (j + 1) * ts)
        xs = xc_ref[rows, 0:SSD_WIDTH]
        bm = xc_ref[rows, SSD_WIDTH:SSD_WIDTH + SSD_GROUPS * n].astype(BF16)
        cm = xc_ref[rows, SSD_WIDTH + SSD_GROUPS * n:CONV_CH].astype(BF16)

        pre = dtr_ref[rows, :] + dtb_ref[...]
        dt = jnp.maximum(pre, 0.0) + jnp.log1p(jnp.exp(-jnp.abs(pre)))
        da = dt * a_neg
        acum = _dot_exact01(tri, da, left=True)
        acum_t = acum.T
        dt_e = _dot_exact01(expand, dt, left=False)
        from_start = _dot_exact01(expand, jnp.exp(acum), left=False)
        to_end = _dot_exact01(expand, jnp.exp(acum[ts - 1:ts, :] - acum), left=False)
        total = from_start[ts - 1:ts, :]

        xdt = xs * dt_e
        xdt_te = (xdt * to_end).astype(BF16)

        zf = z_ref[rows, :].astype(F32)
        gate = _silu(zf)
        for g in range(SSD_GROUPS):
            cg = cm[:, g * n:(g + 1) * n]
            bg = bm[:, g * n:(g + 1) * n]
            cb = lax.dot_general(cg, bg, (((1,), (1,)), ((), ())), preferred_element_type=F32)
            intra = []
            for pr in range(gw // LANES):
                lanes = slice(g * gw + pr * LANES, g * gw + (pr + 1) * LANES)
                xpair = xdt[:, lanes].astype(BF16)
                terms = []
                for half in range(2):
                    hd = (g * gw + pr * LANES) // SSD_HEAD_DIM + half
                    seg = acum[:, hd:hd + 1] - acum_t[hd:hd + 1, :]
                    decay = jnp.exp(jnp.where(causal, seg, -1e30))
                    wmat = (cb * decay).astype(BF16)
                    terms.append(_dot(wmat, xpair))
                intra.append(jnp.where(lane_lo, terms[0], terms[1]))
            y_intra = jnp.concatenate(intra, axis=1)
            gl = slice(g * gw, (g + 1) * gw)
            st = state_ref[:, gl]
            y_inter = _dot(cg, st.astype(BF16)) * from_start[:, gl]
            upd = lax.dot_general(bg, xdt_te[:, gl], (((0,), (0,)), ((), ())), preferred_element_type=F32)
            state_ref[:, gl] = st * total[:, gl] + upd
            y = (y_intra + y_inter + xs[:, gl] * dexp_ref[:, gl]) * gate[:, gl]
            ms = jnp.mean(y * y, axis=-1, keepdims=True)
            o_ref[rows, gl] = (y * lax.rsqrt(ms + LN_EPS) * ngain_ref[:, gl]).astype(o_ref.dtype)


N_RET_IN = 9
N_SSD_IN = 10


def _mixer_kernel(*refs):
    ret_in = refs[:N_RET_IN]
    ssd_in = refs[N_RET_IN:N_RET_IN + N_SSD_IN]
    ret_o, ssd_o, ret_state, xpad_ref, ssd_state, xc_ref = refs[N_RET_IN + N_SSD_IN:]

    @pl.when(pl.program_id(1) == 0)
    def _():
        ret_state[...] = jnp.zeros_like(ret_state)
        xpad_ref[0:SUBLANES, :] = jnp.zeros((SUBLANES, CONV_CH), F32)
        ssd_state[...] = jnp.zeros_like(ssd_state)

    z_ref, xbc_ref, dtr_ref, convw_ref, convb_ref, dtb_ref, alog_ref, dexp_ref, ngain_ref, expand_ref = ssd_in
    for c in range(xbc_ref.shape[0] // RET_CHUNK):
        _retention_chunk(slice(c * RET_CHUNK, (c + 1) * RET_CHUNK), *ret_in, ret_o, ret_state)
    _ssd_conv(xbc_ref, convw_ref, convb_ref, xpad_ref, xc_ref)
    _ssd_chunks(range(xbc_ref.shape[0] // SSD_CHUNK), z_ref, dtr_ref, dtb_ref, alog_ref, dexp_ref, ngain_ref,
                expand_ref, ssd_o, ssd_state, xc_ref)


def _mixer(qk, rest, dtr, dmask, qd, kd, ret_gain, ret_bias, convw, convb, dtb, a_log, dexp, ngain, expand,
           batch, seq):
    t = MIX_STEP
    nc = seq // t
    w = RET_WIDTH
    row = lambda b, c: b * nc + c
    const2 = lambda b, c: (0, 0)
    return pl.pallas_call(
        _mixer_kernel,
        grid=(batch, nc),
        in_specs=[
            pl.BlockSpec((t, w), lambda b, c: (row(b, c), 0)),
            pl.BlockSpec((t, w), lambda b, c: (row(b, c), 1)),
            pl.BlockSpec((t, w), lambda b, c: (row(b, c), 0)),
            pl.BlockSpec((t, w), lambda b, c: (row(b, c), 1)),
            pl.BlockSpec((RET_HEADS, RET_CHUNK, RET_CHUNK), lambda b, c: (0, 0, 0)),
            pl.BlockSpec((RET_CHUNK, w), const2),
            pl.BlockSpec((RET_CHUNK, w), const2),
            pl.BlockSpec((1, w), const2),
            pl.BlockSpec((1, w), const2),
            pl.BlockSpec((t, SSD_WIDTH), lambda b, c: (row(b, c), 2)),
            pl.BlockSpec((t, CONV_CH), lambda b, c: (row(b, c), 2)),
            pl.BlockSpec((t, LANES), lambda b, c: (row(b, c), 0)),
            pl.BlockSpec((CONV_WIDTH, CONV_CH), const2),
            pl.BlockSpec((1, CONV_CH), const2),
            pl.BlockSpec((1, LANES), const2),
            pl.BlockSpec((1, LANES), const2),
            pl.BlockSpec((1, SSD_WIDTH), const2),
            pl.BlockSpec((1, SSD_WIDTH), const2),
            pl.BlockSpec((LANES, SSD_WIDTH), const2),
        ],
        out_specs=[
            pl.BlockSpec((t, w), lambda b, c: (row(b, c), 0)),
            pl.BlockSpec((t, SSD_WIDTH), lambda b, c: (row(b, c), 0)),
        ],
        out_shape=[
            jax.ShapeDtypeStruct((batch * seq, w), BF16),
            jax.ShapeDtypeStruct((batch * seq, SSD_WIDTH), BF16),
        ],
        scratch_shapes=[
            pltpu.VMEM((RET_HEADS, RET_HEAD_DIM, RET_HEAD_DIM), F32),
            pltpu.VMEM((t + SUBLANES, CONV_CH), F32),
            pltpu.VMEM((SSD_STATE, SSD_WIDTH), F32),
            pltpu.VMEM((t, CONV_CH), F32),
        ],
        compiler_params=_cparams(("parallel", "arbitrary")),
        name="mixer",
    )(qk, qk, rest, rest, dmask, qd, kd, ret_gain, ret_bias,
      rest, rest, dtr, convw, convb, dtb, a_log, dexp, ngain, expand)


def _outproj_ln_kernel(ret_ref, ssd_ref, x_ref, wo_ref, gain_ref, bias_ref, o_ref, *, alpha):
    for r in range(OUT_TM // OUT_ROWS):
        rows = slice(r * OUT_ROWS, (r + 1) * OUT_ROWS)
        mix = _dot(ret_ref[rows, :], wo_ref[0:RET_WIDTH, :]) + _dot(ssd_ref[rows, :], wo_ref[RET_WIDTH:, :])
        y = alpha * x_ref[rows, :] + mix
        o_ref[rows, :] = _layer_norm_rows(y, gain_ref[...], bias_ref[...])


def _outproj_ln(ret, ssd, x, wo, gain, bias, alpha):
    m, d = x.shape
    return pl.pallas_call(
        functools.partial(_outproj_ln_kernel, alpha=alpha),
        grid=(m // OUT_TM,),
        in_specs=[
            pl.BlockSpec((OUT_TM, RET_WIDTH), lambda i: (i, 0)),
            pl.BlockSpec((OUT_TM, SSD_WIDTH), lambda i: (i, 0)),
            pl.BlockSpec((OUT_TM, d), lambda i: (i, 0)),
            pl.BlockSpec((RET_WIDTH + SSD_WIDTH, d), lambda i: (0, 0), pipeline_mode=pl.Buffered(1)),
            pl.BlockSpec((1, d), lambda i: (0, 0)),
            pl.BlockSpec((1, d), lambda i: (0, 0)),
        ],
        out_specs=pl.BlockSpec((OUT_TM, d), lambda i: (i, 0)),
        out_shape=jax.ShapeDtypeStruct((m, d), F32),
        compiler_params=_cparams(("parallel",)),
        name="outproj_ln",
    )(ret, ssd, x, wo, gain, bias)


def _w_in_prep_kernel(wt_ref, perm_ref, qk_ref, rest_ref, dt_ref, *, n_qk, n_rest):
    j = pl.program_id(0)

    @pl.when(j < n_qk)
    def _():
        reordered = _dot(perm_ref[...], wt_ref[...].astype(BF16))
        qk_ref[...] = reordered.T.astype(BF16)

    @pl.when(jnp.logical_and(j >= n_qk, j < n_qk + n_rest))
    def _():
        rest_ref[...] = wt_ref[...].T.astype(BF16)

    @pl.when(j == n_qk + n_rest)
    def _():
        w_dt = wt_ref[0:SSD_HEADS, :].astype(BF16)
        r = lax.broadcasted_iota(jnp.int32, (SSD_HEADS, LANES), 0)
        l = lax.broadcasted_iota(jnp.int32, (SSD_HEADS, LANES), 1)
        embed = jnp.where(r == l, 1.0, 0.0).astype(BF16)
        dt_ref[...] = lax.dot_general(w_dt, embed, (((0,), (0,)), ((), ())),
                                      preferred_element_type=F32).astype(BF16)


def _w_in_prep(w_in_layers, layer, perm_t):
    wt = jnp.swapaxes(w_in_layers, 1, 2)
    _, n, k = wt.shape
    rows = W_PREP_ROWS
    n_qk = 2 * RET_WIDTH // rows
    n_rest = (2 * RET_WIDTH + SSD_WIDTH + CONV_CH) // rows
    return pl.pallas_call(
        functools.partial(_w_in_prep_kernel, n_qk=n_qk, n_rest=n_rest),
        grid=(n_qk + n_rest + 1,),
        in_specs=[
            pl.BlockSpec((None, rows, k), lambda j: (layer, j, 0)),
            pl.BlockSpec((rows, rows), lambda j: (0, 0)),
        ],
        out_specs=[
            pl.BlockSpec((k, rows), lambda j: (0, jnp.minimum(j, n_qk - 1))),
            pl.BlockSpec((k, rows), lambda j: (0, jnp.clip(j - n_qk, 0, n_rest - 1))),
            pl.BlockSpec((k, LANES), lambda j: (0, 0)),
        ],
        out_shape=[
            jax.ShapeDtypeStruct((k, n_qk * rows), BF16),
            jax.ShapeDtypeStruct((k, n_rest * rows), BF16),
            jax.ShapeDtypeStruct((k, LANES), BF16),
        ],
        compiler_params=_cparams(("arbitrary",)),
        name="w_in_prep",
    )(wt, perm_t)


def _even_odd_permutation_t():
    half = RET_HEAD_DIM // 2
    j = jnp.arange(RET_HEAD_DIM)
    src = jnp.where(j < half, 2 * j, 2 * (j - half) + 1)
    return (src[:, None] == jnp.arange(RET_HEAD_DIM)[None, :]).astype(BF16)


def kernel(x, positions, ffn1_w_gate, ffn1_w_up, ffn1_w_down, ln1_gain, ln1_bias, mix_w_in, ret_gn_gain, ret_gn_bias, ssd_conv_w, ssd_conv_b, ssd_dt_bias, ssd_a_log, ssd_d, ssd_norm_gain, mix_w_out, ln2_gain, ln2_bias, ffn2_w_gate, ffn2_w_up, ffn2_w_down, ln3_gain, ln3_bias):
    batch, seq, d = x.shape
    depth = ffn1_w_gate.shape[0]
    alpha = (2.0 * depth) ** 0.25
    m = batch * seq
    h = x.reshape(m, d)

    half = RET_HEAD_DIM // 2
    inv_freq = (1.0 / (ROPE_BASE ** jnp.linspace(0.0, 1.0, half, dtype=F32))).reshape(1, half)
    cos, sin = _rope_tables(positions.reshape(m, 1), inv_freq)
    dmask, qd, kd = _retention_consts(RET_CHUNK)
    perm_t = _even_odd_permutation_t()
    expand = (jnp.arange(SSD_WIDTH)[None, :] // SSD_HEAD_DIM == jnp.arange(LANES)[:, None]).astype(BF16)
    row = lambda v: v.reshape(1, -1).astype(F32)
    pad_lanes = lambda v: jnp.pad(v.astype(F32), (0, LANES - v.shape[0])).reshape(1, LANES)

    for layer in range(depth):
        w_qk, w_rest, w_dt = _w_in_prep(mix_w_in, layer, perm_t)

        h, ffn2_w = _ffn_ln(h, ffn1_w_gate[layer].astype(BF16), ffn1_w_up[layer].astype(BF16),
                            ffn1_w_down[layer].astype(BF16), row(ln1_gain[layer]), row(ln1_bias[layer]), alpha,
                            later_ffn=(ffn2_w_gate, ffn2_w_up, ffn2_w_down, layer))

        qk, rest, dtr = _in_proj(h, w_qk, w_rest, w_dt, cos, sin)

        dexp = jnp.repeat(ssd_d[layer].astype(F32), SSD_HEAD_DIM).reshape(1, SSD_WIDTH)
        ret, ssd = _mixer(qk, rest, dtr, dmask, qd, kd, row(ret_gn_gain[layer]), row(ret_gn_bias[layer]),
                          ssd_conv_w[layer].astype(F32), row(ssd_conv_b[layer]), pad_lanes(ssd_dt_bias[layer]),
                          pad_lanes(ssd_a_log[layer]), dexp, row(ssd_norm_gain[layer]), expand, batch, seq)

        h = _outproj_ln(ret, ssd, h, mix_w_out[layer].astype(BF16), row(ln2_gain[layer]),
                        row(ln2_bias[layer]), alpha)

        h, _ = _ffn_ln(h, *ffn2_w, row(ln3_gain[layer]), row(ln3_bias[layer]), alpha)

    return h.reshape(batch, seq, d)
```

```python
import functools
import math

import jax
import jax.numpy as jnp
from jax import lax
from jax.experimental import pallas as pl
from jax.experimental.pallas import tpu as pltpu

F32 = jnp.float32
BF16 = jnp.bfloat16

D_MODEL = 2048
RET_WIDTH = 1024
RET_HEAD_DIM = 256
RET_HEADS = 4
SSD_WIDTH = 1024
SSD_HEAD_DIM = 64
SSD_HEADS = 16
SSD_GROUPS = 2
SSD_STATE = 128
CONV_WIDTH = 4
CONV_CH = SSD_WIDTH + 2 * SSD_GROUPS * SSD_STATE
D_FF = 5632
ROPE_BASE = 10000.0
LN_EPS = 1e-5
FFN_RES_WEIGHT = 0.5

LANES = 128
SUBLANES = 8
VMEM_BYTES_V7X = 64 * 1024 * 1024
VMEM_LIMIT = VMEM_BYTES_V7X - 4 * 1024 * 1024
VMEM_LIMIT_FFN = VMEM_BYTES_V7X - 1024 * 1024

FFN_TM = 1024
FFN_TF = 512
FFN_ROWS = 512
FFN_LN_ROWS = 256
PROJ_TM = 1024
PROJ_TN_QK = 1024
PROJ_TN_REST = 768
OUT_TM = 1024
OUT_ROWS = 128
RET_CHUNK = 256
SSD_CHUNK = 128
MIX_STEP = 512
ROPE_TM = 1024
W_PREP_ROWS = 256


def _cparams(sem, vmem_limit=VMEM_LIMIT):
    return pltpu.CompilerParams(dimension_semantics=sem, vmem_limit_bytes=vmem_limit)


def _silu(v):
    h = 0.5 * v
    return h + h * jnp.tanh(h)


def _layer_norm_rows(y, gain, bias):
    mu = jnp.mean(y, axis=-1, keepdims=True)
    d = y - mu
    var = jnp.mean(d * d, axis=-1, keepdims=True)
    return d * lax.rsqrt(var + LN_EPS) * gain + bias


def _split3(v):
    h0 = v.astype(BF16)
    r1 = v - h0.astype(F32)
    h1 = r1.astype(BF16)
    h2 = (r1 - h1.astype(F32)).astype(BF16)
    return h0, h1, h2


def _dot(a, b):
    return jnp.dot(a, b, preferred_element_type=F32)


def _dot_exact01(m01, v, left=True):
    parts = _split3(v)
    if left:
        return _dot(m01, parts[0]) + _dot(m01, parts[1]) + _dot(m01, parts[2])
    return _dot(parts[0], m01) + _dot(parts[1], m01) + _dot(parts[2], m01)


def _ffn_ln_kernel(*refs, alpha, n_f, n_cast):
    x_ref, wg_ref, wu_ref, wd_ref, gain_ref, bias_ref = refs[:6]
    cast_in = refs[6:6 + n_cast]
    o_ref = refs[6 + n_cast]
    cast_out = refs[7 + n_cast:7 + 2 * n_cast]
    xb_ref = refs[7 + 2 * n_cast]
    f = pl.program_id(1)

    for src, dst in zip(cast_in, cast_out):
        dst[...] = src[...].astype(dst.dtype)

    def partial_out(rows):
        xb = xb_ref[rows, :]
        h = (_silu(_dot(xb, wg_ref[...])) * _dot(xb, wu_ref[...])).astype(BF16)
        return _dot(h, wd_ref[...])

    @pl.when(f == 0)
    def _():
        for r in range(FFN_TM // FFN_ROWS):
            rows = slice(r * FFN_ROWS, (r + 1) * FFN_ROWS)
            x = x_ref[rows, :]
            xb_ref[rows, :] = x.astype(BF16)
            o_ref[rows, :] = partial_out(rows) + (alpha / FFN_RES_WEIGHT) * x

    @pl.when(jnp.logical_and(f > 0, f < n_f - 1))
    def _():
        for r in range(FFN_TM // FFN_ROWS):
            rows = slice(r * FFN_ROWS, (r + 1) * FFN_ROWS)
            o_ref[rows, :] += partial_out(rows)

    @pl.when(f == n_f - 1)
    def _():
        for r in range(FFN_TM // FFN_LN_ROWS):
            rows = slice(r * FFN_LN_ROWS, (r + 1) * FFN_LN_ROWS)
            y = FFN_RES_WEIGHT * (o_ref[rows, :] + partial_out(rows))
            o_ref[rows, :] = _layer_norm_rows(y, gain_ref[...], bias_ref[...])


def _ffn_ln(x, wg, wu, wd, gain, bias, alpha, later_ffn=None):
    m, d = x.shape
    n_f = D_FF // FFN_TF
    n_i = m // FFN_TM
    cast_args, cast_in_specs, cast_out_specs, cast_shapes = [], [], [], []
    if later_ffn is not None:
        lg, lu, ld, layer = later_ffn
        gu_block = (d // n_i, FFN_TF)
        dn_block = (D_FF // (n_i * n_f), d)
        gu_idx = lambda i, f: (i, f)
        dn_idx = lambda i, f: (i * n_f + f, 0)
        for w, block, idx in ((lg, gu_block, gu_idx), (lu, gu_block, gu_idx), (ld, dn_block, dn_idx)):
            cast_args.append(w)
            cast_in_specs.append(pl.BlockSpec((None,) + block, lambda i, f, idx=idx: (layer,) + idx(i, f)))
            cast_out_specs.append(pl.BlockSpec(block, idx))
            cast_shapes.append(jax.ShapeDtypeStruct(w.shape[1:], BF16))
    outs = pl.pallas_call(
        functools.partial(_ffn_ln_kernel, alpha=alpha, n_f=n_f, n_cast=len(cast_args)),
        grid=(n_i, n_f),
        in_specs=[
            pl.BlockSpec((FFN_TM, d), lambda i, f: (i, 0)),
            pl.BlockSpec((d, FFN_TF), lambda i, f: (0, f)),
            pl.BlockSpec((d, FFN_TF), lambda i, f: (0, f)),
            pl.BlockSpec((FFN_TF, d), lambda i, f: (f, 0)),
            pl.BlockSpec((1, d), lambda i, f: (0, 0)),
            pl.BlockSpec((1, d), lambda i, f: (0, 0)),
        ] + cast_in_specs,
        out_specs=[pl.BlockSpec((FFN_TM, d), lambda i, f: (i, 0))] + cast_out_specs,
        out_shape=[jax.ShapeDtypeStruct((m, d), F32)] + cast_shapes,
        scratch_shapes=[pltpu.VMEM((FFN_TM, d), BF16)],
        compiler_params=_cparams(("parallel", "arbitrary"), VMEM_LIMIT_FFN),
        name="ffn_ln",
    )(x, wg, wu, wd, gain, bias, *cast_args)
    return outs[0], tuple(outs[1:])


def _rope_tab_kernel(pos_ref, invf_ref, cos_ref, sin_ref):
    theta = pos_ref[...].astype(F32) * invf_ref[...]
    cos_ref[...] = jnp.cos(theta)
    sin_ref[...] = jnp.sin(theta)


def _rope_tables(pos_col, inv_freq):
    m = pos_col.shape[0]
    half = inv_freq.shape[1]
    return pl.pallas_call(
        _rope_tab_kernel,
        grid=(m // ROPE_TM,),
        in_specs=[
            pl.BlockSpec((ROPE_TM, 1), lambda i: (i, 0)),
            pl.BlockSpec((1, half), lambda i: (0, 0)),
        ],
        out_specs=[
            pl.BlockSpec((ROPE_TM, half), lambda i: (i, 0)),
            pl.BlockSpec((ROPE_TM, half), lambda i: (i, 0)),
        ],
        out_shape=[jax.ShapeDtypeStruct((m, half), F32)] * 2,
        compiler_params=_cparams(("parallel",)),
        name="rope_tab",
    )(pos_col, inv_freq)


def _in_proj_kernel(xa_ref, xb_half_ref, wqk_ref, wrest_ref, wdt_ref, cos_ref, sin_ref, qk_ref, rest_ref, dt_ref,
                    xb_ref, *, n_qk, k_scale):
    j = pl.program_id(1)

    @pl.when(j == 0)
    def _():
        hr = xa_ref.shape[0]
        xb_ref[0:hr, :] = xa_ref[...].astype(BF16)
        xb_ref[hr:, :] = xb_half_ref[...].astype(BF16)
        dt_ref[...] = _dot(xb_ref[...], wdt_ref[...])

    @pl.when(j < n_qk)
    def _():
        acc = _dot(xb_ref[...], wqk_ref[...])
        scale = jnp.where(j >= n_qk // 2, k_scale, 1.0).astype(F32)
        c = cos_ref[...] * scale
        s = sin_ref[...] * scale
        half = RET_HEAD_DIM // 2
        for hd in range(acc.shape[1] // RET_HEAD_DIM):
            lo = hd * RET_HEAD_DIM
            e = acc[:, lo:lo + half]
            o = acc[:, lo + half:lo + RET_HEAD_DIM]
            qk_ref[:, lo:lo + half] = (e * c - o * s).astype(qk_ref.dtype)
            qk_ref[:, lo + half:lo + RET_HEAD_DIM] = (o * c + e * s).astype(qk_ref.dtype)

    @pl.when(j >= n_qk)
    def _():
        rest_ref[...] = _dot(xb_ref[...], wrest_ref[...]).astype(rest_ref.dtype)


def _in_proj(x, w_qk, w_rest, w_dt, cos, sin):
    m, k = x.shape
    half = RET_HEAD_DIM // 2
    tn_qk, tn_rest = PROJ_TN_QK, PROJ_TN_REST
    n_qk = w_qk.shape[1] // tn_qk
    n_rest = w_rest.shape[1] // tn_rest
    qk_j = lambda j: jnp.minimum(j, n_qk - 1)
    rest_j = lambda j: jnp.maximum(j - n_qk, 0)
    n_i = m // PROJ_TM
    n_j = n_qk + n_rest
    half_rows = lambda h, switch: (lambda i, j: (2 * jnp.minimum(i + (j >= switch), n_i - 1) + h, 0))
    return pl.pallas_call(
        functools.partial(_in_proj_kernel, n_qk=n_qk, k_scale=RET_HEAD_DIM ** -0.5),
        grid=(n_i, n_j),
        in_specs=[
            pl.BlockSpec((PROJ_TM // 2, k), half_rows(0, n_j // 3)),
            pl.BlockSpec((PROJ_TM // 2, k), half_rows(1, (2 * n_j) // 3)),
            pl.BlockSpec((k, tn_qk), lambda i, j: (0, qk_j(j))),
            pl.BlockSpec((k, tn_rest), lambda i, j: (0, rest_j(j))),
            pl.BlockSpec((k, LANES), lambda i, j: (0, 0)),
            pl.BlockSpec((PROJ_TM, half), lambda i, j: (i, 0)),
            pl.BlockSpec((PROJ_TM, half), lambda i, j: (i, 0)),
        ],
        out_specs=[
            pl.BlockSpec((PROJ_TM, tn_qk), lambda i, j: (i, qk_j(j))),
            pl.BlockSpec((PROJ_TM, tn_rest), lambda i, j: (i, rest_j(j))),
            pl.BlockSpec((PROJ_TM, LANES), lambda i, j: (i, 0)),
        ],
        out_shape=[
            jax.ShapeDtypeStruct((m, n_qk * tn_qk), BF16),
            jax.ShapeDtypeStruct((m, n_rest * tn_rest), BF16),
            jax.ShapeDtypeStruct((m, LANES), F32),
        ],
        scratch_shapes=[pltpu.VMEM((PROJ_TM, k), BF16)],
        compiler_params=_cparams(("parallel", "arbitrary")),
        name="in_proj",
    )(x, x, w_qk, w_rest, w_dt, cos, sin)


def _ret_log_gamma(h):
    return math.log(1.0 - 2.0 ** (-5.0 - h))


def _retention_chunk(rows, q_ref, k_ref, v_ref, g_ref, dmask_ref, qd_ref, kd_ref, gain_ref, bias_ref,
                     o_ref, state_ref):
    tc = rows.stop - rows.start
    heads = range(RET_HEADS)
    sl = [slice(h * RET_HEAD_DIM, (h + 1) * RET_HEAD_DIM) for h in heads]
    scores = [lax.dot_general(q_ref[rows, sl[h]], k_ref[rows, sl[h]], (((1,), (1,)), ((), ())),
                              preferred_element_type=F32) for h in heads]
    inter = [_dot(q_ref[rows, sl[h]], state_ref[h].astype(BF16)) for h in heads]
    kv = []
    for h in heads:
        vkd = v_ref[rows, sl[h]] * kd_ref[:, sl[h]]
        kv.append(lax.dot_general(k_ref[rows, sl[h]], vkd, (((0,), (0,)), ((), ())), preferred_element_type=F32))
    probs = [(scores[h] * dmask_ref[h]).astype(BF16) for h in heads]
    outs = [_dot(probs[h], v_ref[rows, sl[h]]) + inter[h] * qd_ref[:, sl[h]] for h in heads]
    for h in heads:
        state_ref[h] = state_ref[h] * math.exp(_ret_log_gamma(h) * tc) + kv[h]
    for h in heads:
        o = outs[h]
        mu = jnp.mean(o, axis=-1, keepdims=True)
        d = o - mu
        var = jnp.mean(d * d, axis=-1, keepdims=True)
        on = d * lax.rsqrt(var + LN_EPS) * gain_ref[:, sl[h]] + bias_ref[:, sl[h]]
        o_ref[rows, sl[h]] = (_silu(g_ref[rows, sl[h]].astype(F32)) * on).astype(o_ref.dtype)


def _retention_consts(tc):
    lg = jnp.asarray([_ret_log_gamma(h) for h in range(RET_HEADS)], F32)
    pos = jnp.arange(tc, dtype=F32)
    rel = pos[:, None] - pos[None, :]
    dmask = jnp.where(rel >= 0, jnp.exp(lg[:, None, None] * jnp.maximum(rel, 0.0)), 0.0)
    qd = jnp.exp(lg[None, :] * (pos[:, None] + 1.0))
    kd = jnp.exp(lg[None, :] * (tc - 1.0 - pos[:, None]))
    rep = lambda t: jnp.repeat(t, RET_HEAD_DIM, axis=1)
    return dmask.astype(F32), rep(qd).astype(F32), rep(kd).astype(BF16)


def _ssd_conv(xbc_ref, convw_ref, convb_ref, xpad_ref, xc_ref):
    t = xbc_ref.shape[0]
    xf = xbc_ref[...].astype(F32)
    xpad_ref[SUBLANES:, :] = xf
    conv = convb_ref[...] + convw_ref[CONV_WIDTH - 1:CONV_WIDTH, :] * xf
    for s in range(1, CONV_WIDTH):
        w_s = convw_ref[CONV_WIDTH - 1 - s:CONV_WIDTH - s, :]
        conv = conv + w_s * xpad_ref[SUBLANES - s:SUBLANES - s + t, :]
    xc_ref[...] = _silu(conv)
    xpad_ref[0:SUBLANES, :] = xf[t - SUBLANES:t, :]


def _ssd_chunks(chunks, z_ref, dtr_ref, dtb_ref, alog_ref, dexp_ref, ngain_ref, expand_ref, o_ref, state_ref,
                xc_ref):
    ts = SSD_CHUNK
    n = SSD_STATE
    gw = SSD_WIDTH // SSD_GROUPS
    ri = lax.broadcasted_iota(jnp.int32, (ts, ts), 0)
    ci = lax.broadcasted_iota(jnp.int32, (ts, ts), 1)
    causal = ri >= ci
    tri = jnp.where(causal, 1.0, 0.0).astype(BF16)
    lane_lo = ci < SSD_HEAD_DIM
    expand = expand_ref[...]
    a_neg = -jnp.exp(alog_ref[...])

    for j in chunks:
        rows = slice(j * ts, (j + 1) * ts)
        xs = xc_ref[rows, 0:SSD_WIDTH]
        bm = xc_ref[rows, SSD_WIDTH:SSD_WIDTH + SSD_GROUPS * n].astype(BF16)
        cm = xc_ref[rows, SSD_WIDTH + SSD_GROUPS * n:CONV_CH].astype(BF16)

        pre = dtr_ref[rows, :] + dtb_ref[...]
        dt = jnp.maximum(pre, 0.0) + jnp.log1p(jnp.exp(-jnp.abs(pre)))
        da = dt * a_neg
        acum = _dot_exact01(tri, da, left=True)
        acum_t = acum.T
        dt_e = _dot_exact01(expand, dt, left=False)
        from_start = _dot_exact01(expand, jnp.exp(acum), left=False)
        to_end = _dot_exact01(expand, jnp.exp(acum[ts - 1:ts, :] - acum), left=False)
        total = from_start[ts - 1:ts, :]

        xdt = xs * dt_e
        xdt_te = (xdt * to_end).astype(BF16)

        zf = z_ref[rows, :].astype(F32)
        gate = _silu(zf)
        for g in range(SSD_GROUPS):
            cg = cm[:, g * n:(g + 1) * n]
            bg = bm[:, g * n:(g + 1) * n]
            cb = lax.dot_general(cg, bg, (((1,), (1,)), ((), ())), preferred_element_type=F32)
            intra = []
            for pr in range(gw // LANES):
                lanes = slice(g * gw + pr * LANES, g * gw + (pr + 1) * LANES)
                xpair = xdt[:, lanes].astype(BF16)
                terms = []
                for half in range(2):
                    hd = (g * gw + pr * LANES) // SSD_HEAD_DIM + half
                    seg = acum[:, hd:hd + 1] - acum_t[hd:hd + 1, :]
                    decay = jnp.exp(jnp.where(causal, seg, -1e30))
                    wmat = (cb * decay).astype(BF16)
                    terms.append(_dot(wmat, xpair))
                intra.append(jnp.where(lane_lo, terms[0], terms[1]))
            y_intra = jnp.concatenate(intra, axis=1)
            gl = slice(g * gw, (g + 1) * gw)
            st = state_ref[:, gl]
            y_inter = _dot(cg, st.astype(BF16)) * from_start[:, gl]
            upd = lax.dot_general(bg, xdt_te[:, gl], (((0,), (0,)), ((), ())), preferred_element_type=F32)
            state_ref[:, gl] = st * total[:, gl] + upd
            y = (y_intra + y_inter + xs[:, gl] * dexp_ref[:, gl]) * gate[:, gl]
            ms = jnp.mean(y * y, axis=-1, keepdims=True)
            o_ref[rows, gl] = (y * lax.rsqrt(ms + LN_EPS) * ngain_ref[:, gl]).astype(o_ref.dtype)


N_RET_IN = 9
N_SSD_IN = 10


def _mixer_kernel(*refs):
    ret_in = refs[:N_RET_IN]
    ssd_in = refs[N_RET_IN:N_RET_IN + N_SSD_IN]
    ret_o, ssd_o, ret_state, xpad_ref, ssd_state, xc_ref = refs[N_RET_IN + N_SSD_IN:]

    @pl.when(pl.program_id(1) == 0)
    def _():
        ret_state[...] = jnp.zeros_like(ret_state)
        xpad_ref[0:SUBLANES, :] = jnp.zeros((SUBLANES, CONV_CH), F32)
        ssd_state[...] = jnp.zeros_like(ssd_state)

    z_ref, xbc_ref, dtr_ref, convw_ref, convb_ref, dtb_ref, alog_ref, dexp_ref, ngain_ref, expand_ref = ssd_in
    for c in range(xbc_ref.shape[0] // RET_CHUNK):
        _retention_chunk(slice(c * RET_CHUNK, (c + 1) * RET_CHUNK), *ret_in, ret_o, ret_state)
    _ssd_conv(xbc_ref, convw_ref, convb_ref, xpad_ref, xc_ref)
    _ssd_chunks(range(xbc_ref.shape[0] // SSD_CHUNK), z_ref, dtr_ref, dtb_ref, alog_ref, dexp_ref, ngain_ref,
                expand_ref, ssd_o, ssd_state, xc_ref)


def _mixer(qk, rest, dtr, dmask, qd, kd, ret_gain, ret_bias, convw, convb, dtb, a_log, dexp, ngain, expand,
           batch, seq):
    t = MIX_STEP
    nc = seq // t
    w = RET_WIDTH
    row = lambda b, c: b * nc + c
    const2 = lambda b, c: (0, 0)
    return pl.pallas_call(
        _mixer_kernel,
        grid=(batch, nc),
        in_specs=[
            pl.BlockSpec((t, w), lambda b, c: (row(b, c), 0)),
            pl.BlockSpec((t, w), lambda b, c: (row(b, c), 1)),
            pl.BlockSpec((t, w), lambda b, c: (row(b, c), 0)),
            pl.BlockSpec((t, w), lambda b, c: (row(b, c), 1)),
            pl.BlockSpec((RET_HEADS, RET_CHUNK, RET_CHUNK), lambda b, c: (0, 0, 0)),
            pl.BlockSpec((RET_CHUNK, w), const2),
            pl.BlockSpec((RET_CHUNK, w), const2),
            pl.BlockSpec((1, w), const2),
            pl.BlockSpec((1, w), const2),
            pl.BlockSpec((t, SSD_WIDTH), lambda b, c: (row(b, c), 2)),
            pl.BlockSpec((t, CONV_CH), lambda b, c: (row(b, c), 2)),
            pl.BlockSpec((t, LANES), lambda b, c: (row(b, c), 0)),
            pl.BlockSpec((CONV_WIDTH, CONV_CH), const2),
            pl.BlockSpec((1, CONV_CH), const2),
            pl.BlockSpec((1, LANES), const2),
            pl.BlockSpec((1, LANES), const2),
            pl.BlockSpec((1, SSD_WIDTH), const2),
            pl.BlockSpec((1, SSD_WIDTH), const2),
            pl.BlockSpec((LANES, SSD_WIDTH), const2),
        ],
        out_specs=[
            pl.BlockSpec((t, w), lambda b, c: (row(b, c), 0)),
            pl.BlockSpec((t, SSD_WIDTH), lambda b, c: (row(b, c), 0)),
        ],
        out_shape=[
            jax.ShapeDtypeStruct((batch * seq, w), BF16),
            jax.ShapeDtypeStruct((batch * seq, SSD_WIDTH), BF16),
        ],
        scratch_shapes=[
            pltpu.VMEM((RET_HEADS, RET_HEAD_DIM, RET_HEAD_DIM), F32),
            pltpu.VMEM((t + SUBLANES, CONV_CH), F32),
            pltpu.VMEM((SSD_STATE, SSD_WIDTH), F32),
            pltpu.VMEM((t, CONV_CH), F32),
        ],
        compiler_params=_cparams(("parallel", "arbitrary")),
        name="mixer",
    )(qk, qk, rest, rest, dmask, qd, kd, ret_gain, ret_bias,
      rest, rest, dtr, convw, convb, dtb, a_log, dexp, ngain, expand)


def _outproj_ln_kernel(ret_ref, ssd_ref, x_ref, wo_ref, gain_ref, bias_ref, o_ref, *, alpha):
    for r in range(OUT_TM // OUT_ROWS):
        rows = slice(r * OUT_ROWS, (r + 1) * OUT_ROWS)
        mix = _dot(ret_ref[rows, :], wo_ref[0:RET_WIDTH, :]) + _dot(ssd_ref[rows, :], wo_ref[RET_WIDTH:, :])
        y = alpha * x_ref[rows, :] + mix
        o_ref[rows, :] = _layer_norm_rows(y, gain_ref[...], bias_ref[...])


def _outproj_ln(ret, ssd, x, wo, gain, bias, alpha):
    m, d = x.shape
    return pl.pallas_call(
        functools.partial(_outproj_ln_kernel, alpha=alpha),
        grid=(m // OUT_TM,),
        in_specs=[
            pl.BlockSpec((OUT_TM, RET_WIDTH), lambda i: (i, 0)),
            pl.BlockSpec((OUT_TM, SSD_WIDTH), lambda i: (i, 0)),
            pl.BlockSpec((OUT_TM, d), lambda i: (i, 0)),
            pl.BlockSpec((RET_WIDTH + SSD_WIDTH, d), lambda i: (0, 0), pipeline_mode=pl.Buffered(1)),
            pl.BlockSpec((1, d), lambda i: (0, 0)),
            pl.BlockSpec((1, d), lambda i: (0, 0)),
        ],
        out_specs=pl.BlockSpec((OUT_TM, d), lambda i: (i, 0)),
        out_shape=jax.ShapeDtypeStruct((m, d), F32),
        compiler_params=_cparams(("parallel",)),
        name="outproj_ln",
    )(ret, ssd, x, wo, gain, bias)


def _w_in_prep_kernel(wt_ref, perm_ref, qk_ref, rest_ref, dt_ref, *, n_qk, n_rest):
    j = pl.program_id(0)

    @pl.when(j < n_qk)
    def _():
        reordered = _dot(perm_ref[...], wt_ref[...].astype(BF16))
        qk_ref[...] = reordered.T.astype(BF16)

    @pl.when(jnp.logical_and(j >= n_qk, j < n_qk + n_rest))
    def _():
        rest_ref[...] = wt_ref[...].T.astype(BF16)

    @pl.when(j == n_qk + n_rest)
    def _():
        w_dt = wt_ref[0:SSD_HEADS, :].astype(BF16)
        r = lax.broadcasted_iota(jnp.int32, (SSD_HEADS, LANES), 0)
        l = lax.broadcasted_iota(jnp.int32, (SSD_HEADS, LANES), 1)
        embed = jnp.where(r == l, 1.0, 0.0).astype(BF16)
        dt_ref[...] = lax.dot_general(w_dt, embed, (((0,), (0,)), ((), ())),
                                      preferred_element_type=F32).astype(BF16)


def _w_in_prep(w_in_layers, layer, perm_t):
    wt = jnp.swapaxes(w_in_layers, 1, 2)
    _, n, k = wt.shape
    rows = W_PREP_ROWS
    n_qk = 2 * RET_WIDTH // rows
    n_rest = (2 * RET_WIDTH + SSD_WIDTH + CONV_CH) // rows
    return pl.pallas_call(
        functools.partial(_w_in_prep_kernel, n_qk=n_qk, n_rest=n_rest),
        grid=(n_qk + n_rest + 1,),
        in_specs=[
            pl.BlockSpec((None, rows, k), lambda j: (layer, j, 0)),
            pl.BlockSpec((rows, rows), lambda j: (0, 0)),
        ],
        out_specs=[
            pl.BlockSpec((k, rows), lambda j: (0, jnp.minimum(j, n_qk - 1))),
            pl.BlockSpec((k, rows), lambda j: (0, jnp.clip(j - n_qk, 0, n_rest - 1))),
            pl.BlockSpec((k, LANES), lambda j: (0, 0)),
        ],
        out_shape=[
            jax.ShapeDtypeStruct((k, n_qk * rows), BF16),
            jax.ShapeDtypeStruct((k, n_rest * rows), BF16),
            jax.ShapeDtypeStruct((k, LANES), BF16),
        ],
        compiler_params=_cparams(("arbitrary",)),
        name="w_in_prep",
    )(wt, perm_t)


def _even_odd_permutation_t():
    half = RET_HEAD_DIM // 2
    j = jnp.arange(RET_HEAD_DIM)
    src = jnp.where(j < half, 2 * j, 2 * (j - half) + 1)
    return (src[:, None] == jnp.arange(RET_HEAD_DIM)[None, :]).astype(BF16)


def kernel(x, positions, ffn1_w_gate, ffn1_w_up, ffn1_w_down, ln1_gain, ln1_bias, mix_w_in, ret_gn_gain, ret_gn_bias, ssd_conv_w, ssd_conv_b, ssd_dt_bias, ssd_a_log, ssd_d, ssd_norm_gain, mix_w_out, ln2_gain, ln2_bias, ffn2_w_gate, ffn2_w_up, ffn2_w_down, ln3_gain, ln3_bias):
    batch, seq, d = x.shape
    depth = ffn1_w_gate.shape[0]
    alpha = (2.0 * depth) ** 0.25
    m = batch * seq
    h = x.reshape(m, d)

    half = RET_HEAD_DIM // 2
    inv_freq = (1.0 / (ROPE_BASE ** jnp.linspace(0.0, 1.0, half, dtype=F32))).reshape(1, half)
    cos, sin = _rope_tables(positions.reshape(m, 1), inv_freq)
    dmask, qd, kd = _retention_consts(RET_CHUNK)
    perm_t = _even_odd_permutation_t()
    expand = (jnp.arange(SSD_WIDTH)[None, :] // SSD_HEAD_DIM == jnp.arange(LANES)[:, None]).astype(BF16)
    row = lambda v: v.reshape(1, -1).astype(F32)
    pad_lanes = lambda v: jnp.pad(v.astype(F32), (0, LANES - v.shape[0])).reshape(1, LANES)

    for layer in range(depth):
        w_qk, w_rest, w_dt = _w_in_prep(mix_w_in, layer, perm_t)

        h, ffn2_w = _ffn_ln(h, ffn1_w_gate[layer].astype(BF16), ffn1_w_up[layer].astype(BF16),
                            ffn1_w_down[layer].astype(BF16), row(ln1_gain[layer]), row(ln1_bias[layer]), alpha,
                            later_ffn=(ffn2_w_gate, ffn2_w_up, ffn2_w_down, layer))

        qk, rest, dtr = _in_proj(h, w_qk, w_rest, w_dt, cos, sin)

        dexp = jnp.repeat(ssd_d[layer].astype(F32), SSD_HEAD_DIM).reshape(1, SSD_WIDTH)
        ret, ssd = _mixer(qk, rest, dtr, dmask, qd, kd, row(ret_gn_gain[layer]), row(ret_gn_bias[layer]),
                          ssd_conv_w[layer].astype(F32), row(ssd_conv_b[layer]), pad_lanes(ssd_dt_bias[layer]),
                          pad_lanes(ssd_a_log[layer]), dexp, row(ssd_norm_gain[layer]), expand, batch, seq)

        h = _outproj_ln(ret, ssd, h, mix_w_out[layer].astype(BF16), row(ln2_gain[layer]),
                        row(ln2_bias[layer]), alpha)

        h, _ = _ffn_ln(h, *ffn2_w, row(ln3_gain[layer]), row(ln3_bias[layer]), alpha)

    return h.reshape(batch, seq, d)
```

```python
import functools
import math

import jax
import jax.numpy as jnp
from jax import lax
from jax.experimental import pallas as pl
from jax.experimental.pallas import tpu as pltpu

F32 = jnp.float32
BF16 = jnp.bfloat16

D_MODEL = 2048
RET_WIDTH = 1024
RET_HEAD_DIM = 256
RET_HEADS = 4
SSD_WIDTH = 1024
SSD_HEAD_DIM = 64
SSD_HEADS = 16
SSD_GROUPS = 2
SSD_STATE = 128
CONV_WIDTH = 4
CONV_CH = SSD_WIDTH + 2 * SSD_GROUPS * SSD_STATE
D_FF = 5632
ROPE_BASE = 10000.0
LN_EPS = 1e-5
FFN_RES_WEIGHT = 0.5

LANES = 128
SUBLANES = 8
VMEM_BYTES_V7X = 64 * 1024 * 1024
VMEM_LIMIT = VMEM_BYTES_V7X - 4 * 1024 * 1024
VMEM_LIMIT_FFN = VMEM_BYTES_V7X - 1024 * 1024
MASKED_LOG_DECAY = -1e30

FFN_TM = 1024
FFN_TF = 512
FFN_ROWS = 512
FFN_LN_ROWS = 256
PROJ_TM = 1024
PROJ_TN_QK = 1024
PROJ_TN_REST = 768
OUT_TM = 1024
OUT_ROWS = 128
RET_CHUNK = 256
SSD_CHUNK = 128
MIX_STEP = 512
ROPE_TM = 1024
W_PREP_ROWS = 512


def _cparams(sem, vmem_limit=VMEM_LIMIT):
    return pltpu.CompilerParams(dimension_semantics=sem, vmem_limit_bytes=vmem_limit)


def _silu(v):
    h = 0.5 * v
    return h + h * jnp.tanh(h)


def _layer_norm_rows(y, gain, bias):
    mu = jnp.mean(y, axis=-1, keepdims=True)
    d = y - mu
    var = jnp.mean(d * d, axis=-1, keepdims=True)
    return d * lax.rsqrt(var + LN_EPS) * gain + bias


def _split3(v):
    h0 = v.astype(BF16)
    r1 = v - h0.astype(F32)
    h1 = r1.astype(BF16)
    h2 = (r1 - h1.astype(F32)).astype(BF16)
    return h0, h1, h2


def _dot(a, b):
    return jnp.dot(a, b, preferred_element_type=F32)


def _dot_exact01(m01, v, left=True):
    parts = _split3(v)
    if left:
        return _dot(m01, parts[0]) + _dot(m01, parts[1]) + _dot(m01, parts[2])
    return _dot(parts[0], m01) + _dot(parts[1], m01) + _dot(parts[2], m01)


def _ffn_ln_kernel(*refs, alpha, n_f, n_cast):
    x_ref, wg_ref, wu_ref, wd_ref, gain_ref, bias_ref = refs[:6]
    cast_in = refs[6:6 + n_cast]
    o_ref = refs[6 + n_cast]
    cast_out = refs[7 + n_cast:7 + 2 * n_cast]
    xb_ref = refs[7 + 2 * n_cast]
    f = pl.program_id(1)

    for src, dst in zip(cast_in, cast_out):
        dst[...] = src[...].astype(dst.dtype)

    def partial_out(rows):
        xb = xb_ref[rows, :]
        h = (_silu(_dot(xb, wg_ref[...])) * _dot(xb, wu_ref[...])).astype(BF16)
        return _dot(h, wd_ref[...])

    @pl.when(f == 0)
    def _():
        for r in range(FFN_TM // FFN_ROWS):
            rows = slice(r * FFN_ROWS, (r + 1) * FFN_ROWS)
            x = x_ref[rows, :]
            xb_ref[rows, :] = x.astype(BF16)
            o_ref[rows, :] = partial_out(rows) + (alpha / FFN_RES_WEIGHT) * x

    @pl.when(jnp.logical_and(f > 0, f < n_f - 1))
    def _():
        for r in range(FFN_TM // FFN_ROWS):
            rows = slice(r * FFN_ROWS, (r + 1) * FFN_ROWS)
            o_ref[rows, :] += partial_out(rows)

    @pl.when(f == n_f - 1)
    def _():
        for r in range(FFN_TM // FFN_LN_ROWS):
            rows = slice(r * FFN_LN_ROWS, (r + 1) * FFN_LN_ROWS)
            y = FFN_RES_WEIGHT * (o_ref[rows, :] + partial_out(rows))
            o_ref[rows, :] = _layer_norm_rows(y, gain_ref[...], bias_ref[...])


def _ffn_ln(x, wg, wu, wd, gain, bias, alpha, later_ffn=None):
    m, d = x.shape
    n_f = D_FF // FFN_TF
    n_i = m // FFN_TM
    cast_args, cast_in_specs, cast_out_specs, cast_shapes = [], [], [], []
    if later_ffn is not None:
        lg, lu, ld, layer = later_ffn
        gu_block = (d // n_i, FFN_TF)
        dn_block = (D_FF // (n_i * n_f), d)
        gu_idx = lambda i, f: (i, f)
        dn_idx = lambda i, f: (i * n_f + f, 0)
        for w, block, idx in ((lg, gu_block, gu_idx), (lu, gu_block, gu_idx), (ld, dn_block, dn_idx)):
            cast_args.append(w)
            cast_in_specs.append(pl.BlockSpec((None,) + block, lambda i, f, idx=idx: (layer,) + idx(i, f)))
            cast_out_specs.append(pl.BlockSpec(block, idx))
            cast_shapes.append(jax.ShapeDtypeStruct(w.shape[1:], BF16))
    outs = pl.pallas_call(
        functools.partial(_ffn_ln_kernel, alpha=alpha, n_f=n_f, n_cast=len(cast_args)),
        grid=(n_i, n_f),
        in_specs=[
            pl.BlockSpec((FFN_TM, d), lambda i, f: (i, 0)),
            pl.BlockSpec((d, FFN_TF), lambda i, f: (0, f)),
            pl.BlockSpec((d, FFN_TF), lambda i, f: (0, f)),
            pl.BlockSpec((FFN_TF, d), lambda i, f: (f, 0)),
            pl.BlockSpec((1, d), lambda i, f: (0, 0)),
            pl.BlockSpec((1, d), lambda i, f: (0, 0)),
        ] + cast_in_specs,
        out_specs=[pl.BlockSpec((FFN_TM, d), lambda i, f: (i, 0))] + cast_out_specs,
        out_shape=[jax.ShapeDtypeStruct((m, d), F32)] + cast_shapes,
        scratch_shapes=[pltpu.VMEM((FFN_TM, d), BF16)],
        compiler_params=_cparams(("parallel", "arbitrary"), VMEM_LIMIT_FFN),
        name="ffn_ln",
    )(x, wg, wu, wd, gain, bias, *cast_args)
    return outs[0], tuple(outs[1:])


def _rope_tab_kernel(pos_ref, invf_ref, cos_ref, sin_ref):
    theta = pos_ref[...].astype(F32) * invf_ref[...]
    cos_ref[...] = jnp.cos(theta)
    sin_ref[...] = jnp.sin(theta)


def _rope_tables(pos_col, inv_freq):
    m = pos_col.shape[0]
    half = inv_freq.shape[1]
    return pl.pallas_call(
        _rope_tab_kernel,
        grid=(m // ROPE_TM,),
        in_specs=[
            pl.BlockSpec((ROPE_TM, 1), lambda i: (i, 0)),
            pl.BlockSpec((1, half), lambda i: (0, 0)),
        ],
        out_specs=[
            pl.BlockSpec((ROPE_TM, half), lambda i: (i, 0)),
            pl.BlockSpec((ROPE_TM, half), lambda i: (i, 0)),
        ],
        out_shape=[jax.ShapeDtypeStruct((m, half), F32)] * 2,
        compiler_params=_cparams(("parallel",)),
        name="rope_tab",
    )(pos_col, inv_freq)


def _in_proj_kernel(xa_ref, xb_half_ref, wqk_ref, wrest_ref, wdt_ref, cos_ref, sin_ref, qk_ref, rest_ref, dt_ref,
                    xb_ref, *, n_qk, k_scale):
    j = pl.program_id(1)

    @pl.when(j == 0)
    def _():
        hr = xa_ref.shape[0]
        xb_ref[0:hr, :] = xa_ref[...].astype(BF16)
        xb_ref[hr:, :] = xb_half_ref[...].astype(BF16)
        dt_ref[...] = _dot(xb_ref[...], wdt_ref[...])

    @pl.when(j < n_qk)
    def _():
        acc = _dot(xb_ref[...], wqk_ref[...])
        scale = jnp.where(j >= n_qk // 2, k_scale, 1.0).astype(F32)
        c = cos_ref[...] * scale
        s = sin_ref[...] * scale
        half = RET_HEAD_DIM // 2
        for hd in range(acc.shape[1] // RET_HEAD_DIM):
            lo = hd * RET_HEAD_DIM
            e = acc[:, lo:lo + half]
            o = acc[:, lo + half:lo + RET_HEAD_DIM]
            qk_ref[:, lo:lo + half] = (e * c - o * s).astype(qk_ref.dtype)
            qk_ref[:, lo + half:lo + RET_HEAD_DIM] = (o * c + e * s).astype(qk_ref.dtype)

    @pl.when(j >= n_qk)
    def _():
        rest_ref[...] = _dot(xb_ref[...], wrest_ref[...]).astype(rest_ref.dtype)


def _in_proj(x, w_qk, w_rest, w_dt, cos, sin):
    m, k = x.shape
    half = RET_HEAD_DIM // 2
    tn_qk, tn_rest = PROJ_TN_QK, PROJ_TN_REST
    n_qk = w_qk.shape[1] // tn_qk
    n_rest = w_rest.shape[1] // tn_rest
    qk_j = lambda j: jnp.minimum(j, n_qk - 1)
    rest_j = lambda j: jnp.maximum(j - n_qk, 0)
    n_i = m // PROJ_TM
    n_j = n_qk + n_rest
    half_rows = lambda h, switch: (lambda i, j: (2 * jnp.minimum(i + (j >= switch), n_i - 1) + h, 0))
    return pl.pallas_call(
        functools.partial(_in_proj_kernel, n_qk=n_qk, k_scale=RET_HEAD_DIM ** -0.5),
        grid=(n_i, n_j),
        in_specs=[
            pl.BlockSpec((PROJ_TM // 2, k), half_rows(0, n_j // 3)),
            pl.BlockSpec((PROJ_TM // 2, k), half_rows(1, (2 * n_j) // 3)),
            pl.BlockSpec((k, tn_qk), lambda i, j: (0, qk_j(j))),
            pl.BlockSpec((k, tn_rest), lambda i, j: (0, rest_j(j))),
            pl.BlockSpec((k, LANES), lambda i, j: (0, 0)),
            pl.BlockSpec((PROJ_TM, half), lambda i, j: (i, 0)),
            pl.BlockSpec((PROJ_TM, half), lambda i, j: (i, 0)),
        ],
        out_specs=[
            pl.BlockSpec((PROJ_TM, tn_qk), lambda i, j: (i, qk_j(j))),
            pl.BlockSpec((PROJ_TM, tn_rest), lambda i, j: (i, rest_j(j))),
            pl.BlockSpec((PROJ_TM, LANES), lambda i, j: (i, 0)),
        ],
        out_shape=[
            jax.ShapeDtypeStruct((m, n_qk * tn_qk), BF16),
            jax.ShapeDtypeStruct((m, n_rest * tn_rest), BF16),
            jax.ShapeDtypeStruct((m, LANES), F32),
        ],
        scratch_shapes=[pltpu.VMEM((PROJ_TM, k), BF16)],
        compiler_params=_cparams(("parallel", "arbitrary")),
        name="in_proj",
    )(x, x, w_qk, w_rest, w_dt, cos, sin)


def _ret_log_gamma(h):
    return math.log(1.0 - 2.0 ** (-5.0 - h))


def _retention_chunk(rows, q_ref, k_ref, v_ref, g_ref, dmask_ref, qd_ref, kd_ref, gain_ref, bias_ref,
                     o_ref, state_ref):
    tc = rows.stop - rows.start
    heads = range(RET_HEADS)
    sl = [slice(h * RET_HEAD_DIM, (h + 1) * RET_HEAD_DIM) for h in heads]
    scores = [lax.dot_general(q_ref[rows, sl[h]], k_ref[rows, sl[h]], (((1,), (1,)), ((), ())),
                              preferred_element_type=F32) for h in heads]
    inter = [_dot(q_ref[rows, sl[h]], state_ref[h].astype(BF16)) for h in heads]
    kv = []
    for h in heads:
        vkd = v_ref[rows, sl[h]] * kd_ref[:, sl[h]]
        kv.append(lax.dot_general(k_ref[rows, sl[h]], vkd, (((0,), (0,)), ((), ())), preferred_element_type=F32))
    probs = [(scores[h] * dmask_ref[h]).astype(BF16) for h in heads]
    outs = [_dot(probs[h], v_ref[rows, sl[h]]) + inter[h] * qd_ref[:, sl[h]] for h in heads]
    for h in heads:
        state_ref[h] = state_ref[h] * math.exp(_ret_log_gamma(h) * tc) + kv[h]
    for h in heads:
        o = outs[h]
        mu = jnp.mean(o, axis=-1, keepdims=True)
        d = o - mu
        var = jnp.mean(d * d, axis=-1, keepdims=True)
        on = d * lax.rsqrt(var + LN_EPS) * gain_ref[:, sl[h]] + bias_ref[:, sl[h]]
        o_ref[rows, sl[h]] = (_silu(g_ref[rows, sl[h]].astype(F32)) * on).astype(o_ref.dtype)


def _retention_consts(tc):
    lg = jnp.asarray([_ret_log_gamma(h) for h in range(RET_HEADS)], F32)
    pos = jnp.arange(tc, dtype=F32)
    rel = pos[:, None] - pos[None, :]
    dmask = jnp.where(rel >= 0, jnp.exp(lg[:, None, None] * jnp.maximum(rel, 0.0)), 0.0)
    qd = jnp.exp(lg[None, :] * (pos[:, None] + 1.0))
    kd = jnp.exp(lg[None, :] * (tc - 1.0 - pos[:, None]))
    rep = lambda t: jnp.repeat(t, RET_HEAD_DIM, axis=1)
    return dmask.astype(F32), rep(qd).astype(F32), rep(kd).astype(BF16)


def _ssd_conv(xbc_ref, convw_ref, convb_ref, xpad_ref, xc_ref):
    t = xbc_ref.shape[0]
    xf = xbc_ref[...].astype(F32)
    xpad_ref[SUBLANES:, :] = xf
    conv = convb_ref[...] + convw_ref[CONV_WIDTH - 1:CONV_WIDTH, :] * xf
    for s in range(1, CONV_WIDTH):
        w_s = convw_ref[CONV_WIDTH - 1 - s:CONV_WIDTH - s, :]
        conv = conv + w_s * xpad_ref[SUBLANES - s:SUBLANES - s + t, :]
    xc_ref[...] = _silu(conv)
    xpad_ref[0:SUBLANES, :] = xf[t - SUBLANES:t, :]


def _ssd_chunks(chunks, z_ref, dtr_ref, dtb_ref, alog_ref, dexp_ref, ngain_ref, expand_ref, o_ref, state_ref,
                xc_ref):
    ts = SSD_CHUNK
    n = SSD_STATE
    gw = SSD_WIDTH // SSD_GROUPS
    ri = lax.broadcasted_iota(jnp.int32, (ts, ts), 0)
    ci = lax.broadcasted_iota(jnp.int32, (ts, ts), 1)
    causal = ri >= ci
    tri = jnp.where(causal, 1.0, 0.0).astype(BF16)
    lane_lo = ci < SSD_HEAD_DIM
    expand = expand_ref[...]
    a_neg = -jnp.exp(alog_ref[...])

    for j in chunks:
        rows = slice(j * ts, (j + 1) * ts)
        xs = xc_ref[rows, 0:SSD_WIDTH]
        bm = xc_ref[rows, SSD_WIDTH:SSD_WIDTH + SSD_GROUPS * n].astype(BF16)
        cm = xc_ref[rows, SSD_WIDTH + SSD_GROUPS * n:CONV_CH].astype(BF16)

        pre = dtr_ref[rows, :] + dtb_ref[...]
        dt = jnp.maximum(pre, 0.0) + jnp.log1p(jnp.exp(-jnp.abs(pre)))
        da = dt * a_neg
        acum = _dot_exact01(tri, da, left=True)
        acum_t = acum.T
        dt_e = _dot_exact01(expand, dt, left=False)
        from_start = _dot_exact01(expand, jnp.exp(acum), left=False)
        to_end = _dot_exact01(expand, jnp.exp(acum[ts - 1:ts, :] - acum), left=False)
        total = from_start[ts - 1:ts, :]

        xdt = xs * dt_e
        xdt_te = (xdt * to_end).astype(BF16)

        zf = z_ref[rows, :].astype(F32)
        gate = _silu(zf)
        for g in range(SSD_GROUPS):
            cg = cm[:, g * n:(g + 1) * n]
            bg = bm[:, g * n:(g + 1) * n]
            cb = lax.dot_general(cg, bg, (((1,), (1,)), ((), ())), preferred_element_type=F32)
            intra = []
            for pr in range(gw // LANES):
                lanes = slice(g * gw + pr * LANES, g * gw + (pr + 1) * LANES)
                xpair = xdt[:, lanes].astype(BF16)
                terms = []
                for half in range(2):
                    hd = (g * gw + pr * LANES) // SSD_HEAD_DIM + half
                    seg = acum[:, hd:hd + 1] - acum_t[hd:hd + 1, :]
                    decay = jnp.exp(jnp.where(causal, seg, MASKED_LOG_DECAY))
                    wmat = (cb * decay).astype(BF16)
                    terms.append(_dot(wmat, xpair))
                intra.append(jnp.where(lane_lo, terms[0], terms[1]))
            y_intra = jnp.concatenate(intra, axis=1)
            gl = slice(g * gw, (g + 1) * gw)
            st = state_ref[:, gl]
            y_inter = _dot(cg, st.astype(BF16)) * from_start[:, gl]
            upd = lax.dot_general(bg, xdt_te[:, gl], (((0,), (0,)), ((), ())), preferred_element_type=F32)
            state_ref[:, gl] = st * total[:, gl] + upd
            y = (y_intra + y_inter + xs[:, gl] * dexp_ref[:, gl]) * gate[:, gl]
            ms = jnp.mean(y * y, axis=-1, keepdims=True)
            o_ref[rows, gl] = (y * lax.rsqrt(ms + LN_EPS) * ngain_ref[:, gl]).astype(o_ref.dtype)


N_RET_IN = 9
N_SSD_IN = 10


def _mixer_kernel(*refs):
    ret_in = refs[:N_RET_IN]
    ssd_in = refs[N_RET_IN:N_RET_IN + N_SSD_IN]
    ret_o, ssd_o, ret_state, xpad_ref, ssd_state, xc_ref = refs[N_RET_IN + N_SSD_IN:]

    @pl.when(pl.program_id(1) == 0)
    def _():
        ret_state[...] = jnp.zeros_like(ret_state)
        xpad_ref[0:SUBLANES, :] = jnp.zeros((SUBLANES, CONV_CH), F32)
        ssd_state[...] = jnp.zeros_like(ssd_state)

    z_ref, xbc_ref, dtr_ref, convw_ref, convb_ref, dtb_ref, alog_ref, dexp_ref, ngain_ref, expand_ref = ssd_in
    for c in range(xbc_ref.shape[0] // RET_CHUNK):
        _retention_chunk(slice(c * RET_CHUNK, (c + 1) * RET_CHUNK), *ret_in, ret_o, ret_state)
    _ssd_conv(xbc_ref, convw_ref, convb_ref, xpad_ref, xc_ref)
    _ssd_chunks(range(xbc_ref.shape[0] // SSD_CHUNK), z_ref, dtr_ref, dtb_ref, alog_ref, dexp_ref, ngain_ref,
                expand_ref, ssd_o, ssd_state, xc_ref)


def _mixer(qk, rest, dtr, dmask, qd, kd, ret_gain, ret_bias, convw, convb, dtb, a_log, dexp, ngain, expand,
           batch, seq):
    t = MIX_STEP
    nc = seq // t
    w = RET_WIDTH
    row = lambda b, c: b * nc + c
    const2 = lambda b, c: (0, 0)
    return pl.pallas_call(
        _mixer_kernel,
        grid=(batch, nc),
        in_specs=[
            pl.BlockSpec((t, w), lambda b, c: (row(b, c), 0)),
            pl.BlockSpec((t, w), lambda b, c: (row(b, c), 1)),
            pl.BlockSpec((t, w), lambda b, c: (row(b, c), 0)),
            pl.BlockSpec((t, w), lambda b, c: (row(b, c), 1)),
            pl.BlockSpec((RET_HEADS, RET_CHUNK, RET_CHUNK), lambda b, c: (0, 0, 0)),
            pl.BlockSpec((RET_CHUNK, w), const2),
            pl.BlockSpec((RET_CHUNK, w), const2),
            pl.BlockSpec((1, w), const2),
            pl.BlockSpec((1, w), const2),
            pl.BlockSpec((t, SSD_WIDTH), lambda b, c: (row(b, c), 2)),
            pl.BlockSpec((t, CONV_CH), lambda b, c: (row(b, c), 2)),
            pl.BlockSpec((t, LANES), lambda b, c: (row(b, c), 0)),
            pl.BlockSpec((CONV_WIDTH, CONV_CH), const2),
            pl.BlockSpec((1, CONV_CH), const2),
            pl.BlockSpec((1, LANES), const2),
            pl.BlockSpec((1, LANES), const2),
            pl.BlockSpec((1, SSD_WIDTH), const2),
            pl.BlockSpec((1, SSD_WIDTH), const2),
            pl.BlockSpec((LANES, SSD_WIDTH), const2),
        ],
        out_specs=[
            pl.BlockSpec((t, w), lambda b, c: (row(b, c), 0)),
            pl.BlockSpec((t, SSD_WIDTH), lambda b, c: (row(b, c), 0)),
        ],
        out_shape=[
            jax.ShapeDtypeStruct((batch * seq, w), BF16),
            jax.ShapeDtypeStruct((batch * seq, SSD_WIDTH), BF16),
        ],
        scratch_shapes=[
            pltpu.VMEM((RET_HEADS, RET_HEAD_DIM, RET_HEAD_DIM), F32),
            pltpu.VMEM((t + SUBLANES, CONV_CH), F32),
            pltpu.VMEM((SSD_STATE, SSD_WIDTH), F32),
            pltpu.VMEM((t, CONV_CH), F32),
        ],
        compiler_params=_cparams(("parallel", "arbitrary")),
        name="mixer",
    )(qk, qk, rest, rest, dmask, qd, kd, ret_gain, ret_bias,
      rest, rest, dtr, convw, convb, dtb, a_log, dexp, ngain, expand)


def _outproj_ln_kernel(ret_ref, ssd_ref, x_ref, wo_ref, gain_ref, bias_ref, o_ref, *, alpha):
    for r in range(OUT_TM // OUT_ROWS):
        rows = slice(r * OUT_ROWS, (r + 1) * OUT_ROWS)
        mix = _dot(ret_ref[rows, :], wo_ref[0:RET_WIDTH, :]) + _dot(ssd_ref[rows, :], wo_ref[RET_WIDTH:, :])
        y = alpha * x_ref[rows, :] + mix
        o_ref[rows, :] = _layer_norm_rows(y, gain_ref[...], bias_ref[...])


def _outproj_ln(ret, ssd, x, wo, gain, bias, alpha):
    m, d = x.shape
    return pl.pallas_call(
        functools.partial(_outproj_ln_kernel, alpha=alpha),
        grid=(m // OUT_TM,),
        in_specs=[
            pl.BlockSpec((OUT_TM, RET_WIDTH), lambda i: (i, 0)),
            pl.BlockSpec((OUT_TM, SSD_WIDTH), lambda i: (i, 0)),
            pl.BlockSpec((OUT_TM, d), lambda i: (i, 0)),
            pl.BlockSpec((RET_WIDTH + SSD_WIDTH, d), lambda i: (0, 0), pipeline_mode=pl.Buffered(1)),
            pl.BlockSpec((1, d), lambda i: (0, 0)),
            pl.BlockSpec((1, d), lambda i: (0, 0)),
        ],
        out_specs=pl.BlockSpec((OUT_TM, d), lambda i: (i, 0)),
        out_shape=jax.ShapeDtypeStruct((m, d), F32),
        compiler_params=_cparams(("parallel",)),
        name="outproj_ln",
    )(ret, ssd, x, wo, gain, bias)


def _w_in_prep_kernel(wt_ref, perm_ref, qk_ref, rest_ref, dt_ref, *, n_qk, n_rest):
    j = pl.program_id(0)

    @pl.when(j < n_qk)
    def _():
        reordered = _dot(perm_ref[...], wt_ref[...].astype(BF16))
        qk_ref[...] = reordered.T.astype(BF16)

    @pl.when(jnp.logical_and(j >= n_qk, j < n_qk + n_rest))
    def _():
        rest_ref[...] = wt_ref[...].T.astype(BF16)

    @pl.when(j == n_qk + n_rest)
    def _():
        w_dt = wt_ref[0:SSD_HEADS, :].astype(BF16)
        r = lax.broadcasted_iota(jnp.int32, (SSD_HEADS, LANES), 0)
        l = lax.broadcasted_iota(jnp.int32, (SSD_HEADS, LANES), 1)
        embed = jnp.where(r == l, 1.0, 0.0).astype(BF16)
        dt_ref[...] = lax.dot_general(w_dt, embed, (((0,), (0,)), ((), ())),
                                      preferred_element_type=F32).astype(BF16)


def _w_in_prep(w_in_layers, layer, perm_t):
    wt = jnp.swapaxes(w_in_layers, 1, 2)
    _, n, k = wt.shape
    rows = W_PREP_ROWS
    n_qk = 2 * RET_WIDTH // rows
    n_rest = (2 * RET_WIDTH + SSD_WIDTH + CONV_CH) // rows
    return pl.pallas_call(
        functools.partial(_w_in_prep_kernel, n_qk=n_qk, n_rest=n_rest),
        grid=(n_qk + n_rest + 1,),
        in_specs=[
            pl.BlockSpec((None, rows, k), lambda j: (layer, j, 0)),
            pl.BlockSpec((rows, rows), lambda j: (0, 0)),
        ],
        out_specs=[
            pl.BlockSpec((k, rows), lambda j: (0, jnp.minimum(j, n_qk - 1))),
            pl.BlockSpec((k, rows), lambda j: (0, jnp.clip(j - n_qk, 0, n_rest - 1))),
            pl.BlockSpec((k, LANES), lambda j: (0, 0)),
        ],
        out_shape=[
            jax.ShapeDtypeStruct((k, n_qk * rows), BF16),
            jax.ShapeDtypeStruct((k, n_rest * rows), BF16),
            jax.ShapeDtypeStruct((k, LANES), BF16),
        ],
        compiler_params=_cparams(("arbitrary",)),
        name="w_in_prep",
    )(wt, perm_t)


def _even_odd_permutation_t(n):
    half = RET_HEAD_DIM // 2
    j = jnp.arange(n)
    within = j % RET_HEAD_DIM
    src = (j - within) + jnp.where(within < half, 2 * within, 2 * (within - half) + 1)
    return (src[:, None] == jnp.arange(n)[None, :]).astype(BF16)


def kernel(x, positions, ffn1_w_gate, ffn1_w_up, ffn1_w_down, ln1_gain, ln1_bias, mix_w_in, ret_gn_gain, ret_gn_bias, ssd_conv_w, ssd_conv_b, ssd_dt_bias, ssd_a_log, ssd_d, ssd_norm_gain, mix_w_out, ln2_gain, ln2_bias, ffn2_w_gate, ffn2_w_up, ffn2_w_down, ln3_gain, ln3_bias):
    batch, seq, d = x.shape
    depth = ffn1_w_gate.shape[0]
    alpha = (2.0 * depth) ** 0.25
    m = batch * seq
    h = x.reshape(m, d)

    half = RET_HEAD_DIM // 2
    inv_freq = (1.0 / (ROPE_BASE ** jnp.linspace(0.0, 1.0, half, dtype=F32))).reshape(1, half)
    cos, sin = _rope_tables(positions.reshape(m, 1), inv_freq)
    dmask, qd, kd = _retention_consts(RET_CHUNK)
    perm_t = _even_odd_permutation_t(W_PREP_ROWS)
    expand = (jnp.arange(SSD_WIDTH)[None, :] // SSD_HEAD_DIM == jnp.arange(LANES)[:, None]).astype(BF16)
    row = lambda v: v.reshape(1, -1).astype(F32)
    pad_lanes = lambda v: jnp.pad(v.astype(F32), (0, LANES - v.shape[0])).reshape(1, LANES)

    for layer in range(depth):
        w_qk, w_rest, w_dt = _w_in_prep(mix_w_in, layer, perm_t)

        h, ffn2_w = _ffn_ln(h, ffn1_w_gate[layer].astype(BF16), ffn1_w_up[layer].astype(BF16),
                            ffn1_w_down[layer].astype(BF16), row(ln1_gain[layer]), row(ln1_bias[layer]), alpha,
                            later_ffn=(ffn2_w_gate, ffn2_w_up, ffn2_w_down, layer))

        qk, rest, dtr = _in_proj(h, w_qk, w_rest, w_dt, cos, sin)

        dexp = jnp.repeat(ssd_d[layer].astype(F32), SSD_HEAD_DIM).reshape(1, SSD_WIDTH)
        ret, ssd = _mixer(qk, rest, dtr, dmask, qd, kd, row(ret_gn_gain[layer]), row(ret_gn_bias[layer]),
                          ssd_conv_w[layer].astype(F32), row(ssd_conv_b[layer]), pad_lanes(ssd_dt_bias[layer]),
                          pad_lanes(ssd_a_log[layer]), dexp, row(ssd_norm_gain[layer]), expand, batch, seq)

        h = _outproj_ln(ret, ssd, h, mix_w_out[layer].astype(BF16), row(ln2_gain[layer]),
                        row(ln2_bias[layer]), alpha)

        h, _ = _ffn_ln(h, *ffn2_w, row(ln3_gain[layer]), row(ln3_bias[layer]), alpha)

    return h.reshape(batch, seq, d)
```

```python
import functools
import math

import jax
import jax.numpy as jnp
from jax import lax
from jax.experimental import pallas as pl
from jax.experimental.pallas import tpu as pltpu

F32 = jnp.float32
BF16 = jnp.bfloat16

D_MODEL = 2048
RET_WIDTH = 1024
RET_HEAD_DIM = 256
RET_HEADS = 4
SSD_WIDTH = 1024
SSD_HEAD_DIM = 64
SSD_HEADS = 16
SSD_GROUPS = 2
SSD_STATE = 128
CONV_WIDTH = 4
CONV_CH = SSD_WIDTH + 2 * SSD_GROUPS * SSD_STATE
D_FF = 5632
ROPE_BASE = 10000.0
LN_EPS = 1e-5
FFN_RES_WEIGHT = 0.5

LANES = 128
SUBLANES = 8
VMEM_BYTES_V7X = 64 * 1024 * 1024
VMEM_LIMIT = VMEM_BYTES_V7X - 4 * 1024 * 1024
VMEM_LIMIT_FFN = VMEM_BYTES_V7X - 1024 * 1024
MASKED_LOG_DECAY = -1e30

FFN_TM = 1024
FFN_TF = 512
FFN_ROWS = 512
FFN_LN_ROWS = 256
PROJ_TM = 1024
PROJ_TN_QK = 1024
PROJ_TN_REST = 768
OUT_TM = 1024
OUT_ROWS = 128
RET_CHUNK = 256
SSD_CHUNK = 128
MIX_STEP = 512
ROPE_TM = 1024
W_PREP_ROWS = 512


def _cparams(sem, vmem_limit=VMEM_LIMIT):
    return pltpu.CompilerParams(dimension_semantics=sem, vmem_limit_bytes=vmem_limit)


def _silu(v):
    h = 0.5 * v
    return h + h * jnp.tanh(h)


def _layer_norm_rows(y, gain, bias):
    mu = jnp.mean(y, axis=-1, keepdims=True)
    d = y - mu
    var = jnp.mean(d * d, axis=-1, keepdims=True)
    return d * lax.rsqrt(var + LN_EPS) * gain + bias


def _split3(v):
    h0 = v.astype(BF16)
    r1 = v - h0.astype(F32)
    h1 = r1.astype(BF16)
    h2 = (r1 - h1.astype(F32)).astype(BF16)
    return h0, h1, h2


def _dot(a, b):
    return jnp.dot(a, b, preferred_element_type=F32)


def _dot_exact01(m01, v, left=True):
    parts = _split3(v)
    if left:
        return _dot(m01, parts[0]) + _dot(m01, parts[1]) + _dot(m01, parts[2])
    return _dot(parts[0], m01) + _dot(parts[1], m01) + _dot(parts[2], m01)


def _ffn_ln_kernel(*refs, alpha, n_f, n_cast):
    x_ref, wg_ref, wu_ref, wd_ref, gain_ref, bias_ref = refs[:6]
    cast_in = refs[6:6 + n_cast]
    o_ref = refs[6 + n_cast]
    cast_out = refs[7 + n_cast:7 + 2 * n_cast]
    xb_ref = refs[7 + 2 * n_cast]
    f = pl.program_id(1)

    def cast_slices():
        for src, dst in zip(cast_in, cast_out):
            dst[...] = src[...].astype(dst.dtype)

    def partial_out(rows):
        xb = xb_ref[rows, :]
        h = (_silu(_dot(xb, wg_ref[...])) * _dot(xb, wu_ref[...])).astype(BF16)
        return _dot(h, wd_ref[...])

    @pl.when(f == 0)
    def _():
        cast_slices()
        for r in range(FFN_TM // FFN_ROWS):
            rows = slice(r * FFN_ROWS, (r + 1) * FFN_ROWS)
            x = x_ref[rows, :]
            xb_ref[rows, :] = x.astype(BF16)
            o_ref[rows, :] = partial_out(rows) + (alpha / FFN_RES_WEIGHT) * x

    @pl.when(jnp.logical_and(f > 0, f < n_f - 1))
    def _():
        cast_slices()
        for r in range(FFN_TM // FFN_ROWS):
            rows = slice(r * FFN_ROWS, (r + 1) * FFN_ROWS)
            o_ref[rows, :] += partial_out(rows)

    @pl.when(f == n_f - 1)
    def _():
        cast_slices()
        for r in range(FFN_TM // FFN_LN_ROWS):
            rows = slice(r * FFN_LN_ROWS, (r + 1) * FFN_LN_ROWS)
            y = FFN_RES_WEIGHT * (o_ref[rows, :] + partial_out(rows))
            o_ref[rows, :] = _layer_norm_rows(y, gain_ref[...], bias_ref[...])


def _ffn_ln(x, wg, wu, wd, gain, bias, alpha, later_ffn=None):
    m, d = x.shape
    n_f = D_FF // FFN_TF
    n_i = m // FFN_TM
    cast_args, cast_in_specs, cast_out_specs, cast_shapes = [], [], [], []
    if later_ffn is not None:
        lg, lu, ld, layer = later_ffn
        gu_block = (d // n_i, FFN_TF)
        dn_block = (D_FF // (n_i * n_f), d)
        gu_idx = lambda i, f: (i, f)
        dn_idx = lambda i, f: (i * n_f + f, 0)
        for w, block, idx in ((lg, gu_block, gu_idx), (lu, gu_block, gu_idx), (ld, dn_block, dn_idx)):
            cast_args.append(w)
            cast_in_specs.append(pl.BlockSpec((None,) + block, lambda i, f, idx=idx: (layer,) + idx(i, f)))
            cast_out_specs.append(pl.BlockSpec(block, idx))
            cast_shapes.append(jax.ShapeDtypeStruct(w.shape[1:], BF16))
    outs = pl.pallas_call(
        functools.partial(_ffn_ln_kernel, alpha=alpha, n_f=n_f, n_cast=len(cast_args)),
        grid=(n_i, n_f),
        in_specs=[
            pl.BlockSpec((FFN_TM, d), lambda i, f: (i, 0)),
            pl.BlockSpec((d, FFN_TF), lambda i, f: (0, f)),
            pl.BlockSpec((d, FFN_TF), lambda i, f: (0, f)),
            pl.BlockSpec((FFN_TF, d), lambda i, f: (f, 0)),
            pl.BlockSpec((1, d), lambda i, f: (0, 0)),
            pl.BlockSpec((1, d), lambda i, f: (0, 0)),
        ] + cast_in_specs,
        out_specs=[pl.BlockSpec((FFN_TM, d), lambda i, f: (i, 0))] + cast_out_specs,
        out_shape=[jax.ShapeDtypeStruct((m, d), F32)] + cast_shapes,
        scratch_shapes=[pltpu.VMEM((FFN_TM, d), BF16)],
        compiler_params=_cparams(("parallel", "arbitrary"), VMEM_LIMIT_FFN),
        name="ffn_ln",
    )(x, wg, wu, wd, gain, bias, *cast_args)
    return outs[0], tuple(outs[1:])


def _rope_tab_kernel(pos_ref, invf_ref, cos_ref, sin_ref):
    theta = pos_ref[...].astype(F32) * invf_ref[...]
    cos_ref[...] = jnp.cos(theta)
    sin_ref[...] = jnp.sin(theta)


def _rope_tables(pos_col, inv_freq):
    m = pos_col.shape[0]
    half = inv_freq.shape[1]
    return pl.pallas_call(
        _rope_tab_kernel,
        grid=(m // ROPE_TM,),
        in_specs=[
            pl.BlockSpec((ROPE_TM, 1), lambda i: (i, 0)),
            pl.BlockSpec((1, half), lambda i: (0, 0)),
        ],
        out_specs=[
            pl.BlockSpec((ROPE_TM, half), lambda i: (i, 0)),
            pl.BlockSpec((ROPE_TM, half), lambda i: (i, 0)),
        ],
        out_shape=[jax.ShapeDtypeStruct((m, half), F32)] * 2,
        compiler_params=_cparams(("parallel",)),
        name="rope_tab",
    )(pos_col, inv_freq)


def _in_proj_kernel(xa_ref, xb_half_ref, wqk_ref, wrest_ref, wdt_ref, cos_ref, sin_ref, qk_ref, rest_ref, dt_ref,
                    xb_ref, *, n_qk, k_scale):
    j = pl.program_id(1)

    @pl.when(j == 0)
    def _():
        hr = xa_ref.shape[0]
        xb_ref[0:hr, :] = xa_ref[...].astype(BF16)
        xb_ref[hr:, :] = xb_half_ref[...].astype(BF16)
        dt_ref[...] = _dot(xb_ref[...], wdt_ref[...])

    @pl.when(j < n_qk)
    def _():
        acc = _dot(xb_ref[...], wqk_ref[...])
        scale = jnp.where(j >= n_qk // 2, k_scale, 1.0).astype(F32)
        c = cos_ref[...] * scale
        s = sin_ref[...] * scale
        half = RET_HEAD_DIM // 2
        for hd in range(acc.shape[1] // RET_HEAD_DIM):
            lo = hd * RET_HEAD_DIM
            e = acc[:, lo:lo + half]
            o = acc[:, lo + half:lo + RET_HEAD_DIM]
            qk_ref[:, lo:lo + half] = (e * c - o * s).astype(qk_ref.dtype)
            qk_ref[:, lo + half:lo + RET_HEAD_DIM] = (o * c + e * s).astype(qk_ref.dtype)

    @pl.when(j >= n_qk)
    def _():
        rest_ref[...] = _dot(xb_ref[...], wrest_ref[...]).astype(rest_ref.dtype)


def _in_proj(x, w_qk, w_rest, w_dt, cos, sin):
    m, k = x.shape
    half = RET_HEAD_DIM // 2
    tn_qk, tn_rest = PROJ_TN_QK, PROJ_TN_REST
    n_qk = w_qk.shape[1] // tn_qk
    n_rest = w_rest.shape[1] // tn_rest
    qk_j = lambda j: jnp.minimum(j, n_qk - 1)
    rest_j = lambda j: jnp.maximum(j - n_qk, 0)
    n_i = m // PROJ_TM
    n_j = n_qk + n_rest
    half_rows = lambda h, switch: (lambda i, j: (2 * jnp.minimum(i + (j >= switch), n_i - 1) + h, 0))
    return pl.pallas_call(
        functools.partial(_in_proj_kernel, n_qk=n_qk, k_scale=RET_HEAD_DIM ** -0.5),
        grid=(n_i, n_j),
        in_specs=[
            pl.BlockSpec((PROJ_TM // 2, k), half_rows(0, n_j // 3)),
            pl.BlockSpec((PROJ_TM // 2, k), half_rows(1, (2 * n_j) // 3)),
            pl.BlockSpec((k, tn_qk), lambda i, j: (0, qk_j(j))),
            pl.BlockSpec((k, tn_rest), lambda i, j: (0, rest_j(j))),
            pl.BlockSpec((k, LANES), lambda i, j: (0, 0)),
            pl.BlockSpec((PROJ_TM, half), lambda i, j: (i, 0)),
            pl.BlockSpec((PROJ_TM, half), lambda i, j: (i, 0)),
        ],
        out_specs=[
            pl.BlockSpec((PROJ_TM, tn_qk), lambda i, j: (i, qk_j(j))),
            pl.BlockSpec((PROJ_TM, tn_rest), lambda i, j: (i, rest_j(j))),
            pl.BlockSpec((PROJ_TM, LANES), lambda i, j: (i, 0)),
        ],
        out_shape=[
            jax.ShapeDtypeStruct((m, n_qk * tn_qk), BF16),
            jax.ShapeDtypeStruct((m, n_rest * tn_rest), BF16),
            jax.ShapeDtypeStruct((m, LANES), F32),
        ],
        scratch_shapes=[pltpu.VMEM((PROJ_TM, k), BF16)],
        compiler_params=_cparams(("parallel", "arbitrary")),
        name="in_proj",
    )(x, x, w_qk, w_rest, w_dt, cos, sin)


def _ret_log_gamma(h):
    return math.log(1.0 - 2.0 ** (-5.0 - h))


def _retention_chunk(rows, q_ref, k_ref, v_ref, g_ref, dmask_ref, qd_ref, kd_ref, gain_ref, bias_ref,
                     o_ref, state_ref):
    tc = rows.stop - rows.start
    heads = range(RET_HEADS)
    sl = [slice(h * RET_HEAD_DIM, (h + 1) * RET_HEAD_DIM) for h in heads]
    scores = [lax.dot_general(q_ref[rows, sl[h]], k_ref[rows, sl[h]], (((1,), (1,)), ((), ())),
                              preferred_element_type=F32) for h in heads]
    inter = [_dot(q_ref[rows, sl[h]], state_ref[h].astype(BF16)) for h in heads]
    kv = []
    for h in heads:
        vkd = v_ref[rows, sl[h]] * kd_ref[:, sl[h]]
        kv.append(lax.dot_general(k_ref[rows, sl[h]], vkd, (((0,), (0,)), ((), ())), preferred_element_type=F32))
    probs = [(scores[h] * dmask_ref[h]).astype(BF16) for h in heads]
    outs = [_dot(probs[h], v_ref[rows, sl[h]]) + inter[h] * qd_ref[:, sl[h]] for h in heads]
    for h in heads:
        state_ref[h] = state_ref[h] * math.exp(_ret_log_gamma(h) * tc) + kv[h]
    for h in heads:
        o = outs[h]
        mu = jnp.mean(o, axis=-1, keepdims=True)
        d = o - mu
        var = jnp.mean(d * d, axis=-1, keepdims=True)
        on = d * lax.rsqrt(var + LN_EPS) * gain_ref[:, sl[h]] + bias_ref[:, sl[h]]
        o_ref[rows, sl[h]] = (_silu(g_ref[rows, sl[h]].astype(F32)) * on).astype(o_ref.dtype)


def _retention_consts(tc):
    lg = jnp.asarray([_ret_log_gamma(h) for h in range(RET_HEADS)], F32)
    pos = jnp.arange(tc, dtype=F32)
    rel = pos[:, None] - pos[None, :]
    dmask = jnp.where(rel >= 0, jnp.exp(lg[:, None, None] * jnp.maximum(rel, 0.0)), 0.0)
    qd = jnp.exp(lg[None, :] * (pos[:, None] + 1.0))
    kd = jnp.exp(lg[None, :] * (tc - 1.0 - pos[:, None]))
    rep = lambda t: jnp.repeat(t, RET_HEAD_DIM, axis=1)
    return dmask.astype(F32), rep(qd).astype(F32), rep(kd).astype(BF16)


def _ssd_conv(xbc_ref, convw_ref, convb_ref, xpad_ref, xc_ref):
    t = xbc_ref.shape[0]
    xf = xbc_ref[...].astype(F32)
    xpad_ref[SUBLANES:, :] = xf
    conv = convb_ref[...] + convw_ref[CONV_WIDTH - 1:CONV_WIDTH, :] * xf
    for s in range(1, CONV_WIDTH):
        w_s = convw_ref[CONV_WIDTH - 1 - s:CONV_WIDTH - s, :]
        conv = conv + w_s * xpad_ref[SUBLANES - s:SUBLANES - s + t, :]
    xc_ref[...] = _silu(conv)
    xpad_ref[0:SUBLANES, :] = xf[t - SUBLANES:t, :]


def _ssd_chunks(chunks, z_ref, dtr_ref, dtb_ref, alog_ref, dexp_ref, ngain_ref, expand_ref, o_ref, state_ref,
                xc_ref):
    ts = SSD_CHUNK
    n = SSD_STATE
    gw = SSD_WIDTH // SSD_GROUPS
    ri = lax.broadcasted_iota(jnp.int32, (ts, ts), 0)
    ci = lax.broadcasted_iota(jnp.int32, (ts, ts), 1)
    causal = ri >= ci
    tri = jnp.where(causal, 1.0, 0.0).astype(BF16)
    lane_lo = ci < SSD_HEAD_DIM
    expand = expand_ref[...]
    a_neg = -jnp.exp(alog_ref[...])

    for j in chunks:
        rows = slice(j * ts, (j + 1) * ts)
        xs = xc_ref[rows, 0:SSD_WIDTH]
        bm = xc_ref[rows, SSD_WIDTH:SSD_WIDTH + SSD_GROUPS * n].astype(BF16)
        cm = xc_ref[rows, SSD_WIDTH + SSD_GROUPS * n:CONV_CH].astype(BF16)

        pre = dtr_ref[rows, :] + dtb_ref[...]
        dt = jnp.maximum(pre, 0.0) + jnp.log1p(jnp.exp(-jnp.abs(pre)))
        da = dt * a_neg
        acum = _dot_exact01(tri, da, left=True)
        acum_t = acum.T
        dt_e = _dot_exact01(expand, dt, left=False)
        from_start = _dot_exact01(expand, jnp.exp(acum), left=False)
        to_end = _dot_exact01(expand, jnp.exp(acum[ts - 1:ts, :] - acum), left=False)
        total = from_start[ts - 1:ts, :]

        xdt = xs * dt_e
        xdt_te = (xdt * to_end).astype(BF16)

        zf = z_ref[rows, :].astype(F32)
        gate = _silu(zf)
        for g in range(SSD_GROUPS):
            cg = cm[:, g * n:(g + 1) * n]
            bg = bm[:, g * n:(g + 1) * n]
            cb = lax.dot_general(cg, bg, (((1,), (1,)), ((), ())), preferred_element_type=F32)
            intra = []
            for pr in range(gw // LANES):
                lanes = slice(g * gw + pr * LANES, g * gw + (pr + 1) * LANES)
                xpair = xdt[:, lanes].astype(BF16)
                terms = []
                for half in range(2):
                    hd = (g * gw + pr * LANES) // SSD_HEAD_DIM + half
                    seg = acum[:, hd:hd + 1] - acum_t[hd:hd + 1, :]
                    decay = jnp.exp(jnp.where(causal, seg, MASKED_LOG_DECAY))
                    wmat = (cb * decay).astype(BF16)
                    terms.append(_dot(wmat, xpair))
                intra.append(jnp.where(lane_lo, terms[0], terms[1]))
            y_intra = jnp.concatenate(intra, axis=1)
            gl = slice(g * gw, (g + 1) * gw)
            st = state_ref[:, gl]
            y_inter = _dot(cg, st.astype(BF16)) * from_start[:, gl]
            upd = lax.dot_general(bg, xdt_te[:, gl], (((0,), (0,)), ((), ())), preferred_element_type=F32)
            state_ref[:, gl] = st * total[:, gl] + upd
            y = (y_intra + y_inter + xs[:, gl] * dexp_ref[:, gl]) * gate[:, gl]
            ms = jnp.mean(y * y, axis=-1, keepdims=True)
            o_ref[rows, gl] = (y * lax.rsqrt(ms + LN_EPS) * ngain_ref[:, gl]).astype(o_ref.dtype)


N_RET_IN = 9
N_SSD_IN = 10


def _mixer_kernel(*refs):
    ret_in = refs[:N_RET_IN]
    ssd_in = refs[N_RET_IN:N_RET_IN + N_SSD_IN]
    ret_o, ssd_o, ret_state, xpad_ref, ssd_state, xc_ref = refs[N_RET_IN + N_SSD_IN:]

    @pl.when(pl.program_id(1) == 0)
    def _():
        ret_state[...] = jnp.zeros_like(ret_state)
        xpad_ref[0:SUBLANES, :] = jnp.zeros((SUBLANES, CONV_CH), F32)
        ssd_state[...] = jnp.zeros_like(ssd_state)

    z_ref, xbc_ref, dtr_ref, convw_ref, convb_ref, dtb_ref, alog_ref, dexp_ref, ngain_ref, expand_ref = ssd_in
    for c in range(xbc_ref.shape[0] // RET_CHUNK):
        _retention_chunk(slice(c * RET_CHUNK, (c + 1) * RET_CHUNK), *ret_in, ret_o, ret_state)
    _ssd_conv(xbc_ref, convw_ref, convb_ref, xpad_ref, xc_ref)
    _ssd_chunks(range(xbc_ref.shape[0] // SSD_CHUNK), z_ref, dtr_ref, dtb_ref, alog_ref, dexp_ref, ngain_ref,
                expand_ref, ssd_o, ssd_state, xc_ref)


def _mixer(qk, rest, dtr, dmask, qd, kd, ret_gain, ret_bias, convw, convb, dtb, a_log, dexp, ngain, expand,
           batch, seq):
    t = MIX_STEP
    nc = seq // t
    w = RET_WIDTH
    row = lambda b, c: b * nc + c
    const2 = lambda b, c: (0, 0)
    return pl.pallas_call(
        _mixer_kernel,
        grid=(batch, nc),
        in_specs=[
            pl.BlockSpec((t, w), lambda b, c: (row(b, c), 0)),
            pl.BlockSpec((t, w), lambda b, c: (row(b, c), 1)),
            pl.BlockSpec((t, w), lambda b, c: (row(b, c), 0)),
            pl.BlockSpec((t, w), lambda b, c: (row(b, c), 1)),
            pl.BlockSpec((RET_HEADS, RET_CHUNK, RET_CHUNK), lambda b, c: (0, 0, 0)),
            pl.BlockSpec((RET_CHUNK, w), const2),
            pl.BlockSpec((RET_CHUNK, w), const2),
            pl.BlockSpec((1, w), const2),
            pl.BlockSpec((1, w), const2),
            pl.BlockSpec((t, SSD_WIDTH), lambda b, c: (row(b, c), 2)),
            pl.BlockSpec((t, CONV_CH), lambda b, c: (row(b, c), 2)),
            pl.BlockSpec((t, LANES), lambda b, c: (row(b, c), 0)),
            pl.BlockSpec((CONV_WIDTH, CONV_CH), const2),
            pl.BlockSpec((1, CONV_CH), const2),
            pl.BlockSpec((1, LANES), const2),
            pl.BlockSpec((1, LANES), const2),
            pl.BlockSpec((1, SSD_WIDTH), const2),
            pl.BlockSpec((1, SSD_WIDTH), const2),
            pl.BlockSpec((LANES, SSD_WIDTH), const2),
        ],
        out_specs=[
            pl.BlockSpec((t, w), lambda b, c: (row(b, c), 0)),
            pl.BlockSpec((t, SSD_WIDTH), lambda b, c: (row(b, c), 0)),
        ],
        out_shape=[
            jax.ShapeDtypeStruct((batch * seq, w), BF16),
            jax.ShapeDtypeStruct((batch * seq, SSD_WIDTH), BF16),
        ],
        scratch_shapes=[
            pltpu.VMEM((RET_HEADS, RET_HEAD_DIM, RET_HEAD_DIM), F32),
            pltpu.VMEM((t + SUBLANES, CONV_CH), F32),
            pltpu.VMEM((SSD_STATE, SSD_WIDTH), F32),
            pltpu.VMEM((t, CONV_CH), F32),
        ],
        compiler_params=_cparams(("parallel", "arbitrary")),
        name="mixer",
    )(qk, qk, rest, rest, dmask, qd, kd, ret_gain, ret_bias,
      rest, rest, dtr, convw, convb, dtb, a_log, dexp, ngain, expand)


def _outproj_ln_kernel(ret_ref, ssd_ref, x_ref, wo_ref, gain_ref, bias_ref, o_ref, *, alpha):
    for r in range(OUT_TM // OUT_ROWS):
        rows = slice(r * OUT_ROWS, (r + 1) * OUT_ROWS)
        mix = _dot(ret_ref[rows, :], wo_ref[0:RET_WIDTH, :]) + _dot(ssd_ref[rows, :], wo_ref[RET_WIDTH:, :])
        y = alpha * x_ref[rows, :] + mix
        o_ref[rows, :] = _layer_norm_rows(y, gain_ref[...], bias_ref[...])


def _outproj_ln(ret, ssd, x, wo, gain, bias, alpha):
    m, d = x.shape
    return pl.pallas_call(
        functools.partial(_outproj_ln_kernel, alpha=alpha),
        grid=(m // OUT_TM,),
        in_specs=[
            pl.BlockSpec((OUT_TM, RET_WIDTH), lambda i: (i, 0)),
            pl.BlockSpec((OUT_TM, SSD_WIDTH), lambda i: (i, 0)),
            pl.BlockSpec((OUT_TM, d), lambda i: (i, 0)),
            pl.BlockSpec((RET_WIDTH + SSD_WIDTH, d), lambda i: (0, 0), pipeline_mode=pl.Buffered(1)),
            pl.BlockSpec((1, d), lambda i: (0, 0)),
            pl.BlockSpec((1, d), lambda i: (0, 0)),
        ],
        out_specs=pl.BlockSpec((OUT_TM, d), lambda i: (i, 0)),
        out_shape=jax.ShapeDtypeStruct((m, d), F32),
        compiler_params=_cparams(("parallel",)),
        name="outproj_ln",
    )(ret, ssd, x, wo, gain, bias)


def _w_in_prep_kernel(wt_ref, perm_ref, qk_ref, rest_ref, dt_ref, *, n_qk, n_rest):
    j = pl.program_id(0)

    @pl.when(j < n_qk)
    def _():
        reordered = _dot(perm_ref[...], wt_ref[...].astype(BF16))
        qk_ref[...] = reordered.T.astype(BF16)

    @pl.when(jnp.logical_and(j >= n_qk, j < n_qk + n_rest))
    def _():
        rest_ref[...] = wt_ref[...].T.astype(BF16)

    @pl.when(j == n_qk + n_rest)
    def _():
        w_dt = wt_ref[0:SSD_HEADS, :].astype(BF16)
        r = lax.broadcasted_iota(jnp.int32, (SSD_HEADS, LANES), 0)
        l = lax.broadcasted_iota(jnp.int32, (SSD_HEADS, LANES), 1)
        embed = jnp.where(r == l, 1.0, 0.0).astype(BF16)
        dt_ref[...] = lax.dot_general(w_dt, embed, (((0,), (0,)), ((), ())),
                                      preferred_element_type=F32).astype(BF16)


def _w_in_prep(w_in_layers, layer, perm_t):
    wt = jnp.swapaxes(w_in_layers, 1, 2)
    _, n, k = wt.shape
    rows = W_PREP_ROWS
    n_qk = 2 * RET_WIDTH // rows
    n_rest = (2 * RET_WIDTH + SSD_WIDTH + CONV_CH) // rows
    return pl.pallas_call(
        functools.partial(_w_in_prep_kernel, n_qk=n_qk, n_rest=n_rest),
        grid=(n_qk + n_rest + 1,),
        in_specs=[
            pl.BlockSpec((None, rows, k), lambda j: (layer, j, 0)),
            pl.BlockSpec((rows, rows), lambda j: (0, 0)),
        ],
        out_specs=[
            pl.BlockSpec((k, rows), lambda j: (0, jnp.minimum(j, n_qk - 1))),
            pl.BlockSpec((k, rows), lambda j: (0, jnp.clip(j - n_qk, 0, n_rest - 1))),
            pl.BlockSpec((k, LANES), lambda j: (0, 0)),
        ],
        out_shape=[
            jax.ShapeDtypeStruct((k, n_qk * rows), BF16),
            jax.ShapeDtypeStruct((k, n_rest * rows), BF16),
            jax.ShapeDtypeStruct((k, LANES), BF16),
        ],
        compiler_params=_cparams(("arbitrary",)),
        name="w_in_prep",
    )(wt, perm_t)


def _even_odd_permutation_t(n):
    half = RET_HEAD_DIM // 2
    j = jnp.arange(n)
    within = j % RET_HEAD_DIM
    src = (j - within) + jnp.where(within < half, 2 * within, 2 * (within - half) + 1)
    return (src[:, None] == jnp.arange(n)[None, :]).astype(BF16)


def kernel(x, positions, ffn1_w_gate, ffn1_w_up, ffn1_w_down, ln1_gain, ln1_bias, mix_w_in, ret_gn_gain, ret_gn_bias, ssd_conv_w, ssd_conv_b, ssd_dt_bias, ssd_a_log, ssd_d, ssd_norm_gain, mix_w_out, ln2_gain, ln2_bias, ffn2_w_gate, ffn2_w_up, ffn2_w_down, ln3_gain, ln3_bias):
    batch, seq, d = x.shape
    depth = ffn1_w_gate.shape[0]
    alpha = (2.0 * depth) ** 0.25
    m = batch * seq
    h = x.reshape(m, d)

    half = RET_HEAD_DIM // 2
    inv_freq = (1.0 / (ROPE_BASE ** jnp.linspace(0.0, 1.0, half, dtype=F32))).reshape(1, half)
    cos, sin = _rope_tables(positions.reshape(m, 1), inv_freq)
    dmask, qd, kd = _retention_consts(RET_CHUNK)
    perm_t = _even_odd_permutation_t(W_PREP_ROWS)
    expand = (jnp.arange(SSD_WIDTH)[None, :] // SSD_HEAD_DIM == jnp.arange(LANES)[:, None]).astype(BF16)
    row = lambda v: v.reshape(1, -1).astype(F32)
    pad_lanes = lambda v: jnp.pad(v.astype(F32), (0, LANES - v.shape[0])).reshape(1, LANES)

    for layer in range(depth):
        w_qk, w_rest, w_dt = _w_in_prep(mix_w_in, layer, perm_t)

        h, ffn2_w = _ffn_ln(h, ffn1_w_gate[layer].astype(BF16), ffn1_w_up[layer].astype(BF16),
                            ffn1_w_down[layer].astype(BF16), row(ln1_gain[layer]), row(ln1_bias[layer]), alpha,
                            later_ffn=(ffn2_w_gate, ffn2_w_up, ffn2_w_down, layer))

        qk, rest, dtr = _in_proj(h, w_qk, w_rest, w_dt, cos, sin)

        dexp = jnp.repeat(ssd_d[layer].astype(F32), SSD_HEAD_DIM).reshape(1, SSD_WIDTH)
        ret, ssd = _mixer(qk, rest, dtr, dmask, qd, kd, row(ret_gn_gain[layer]), row(ret_gn_bias[layer]),
                          ssd_conv_w[layer].astype(F32), row(ssd_conv_b[layer]), pad_lanes(ssd_dt_bias[layer]),
                          pad_lanes(ssd_a_log[layer]), dexp, row(ssd_norm_gain[layer]), expand, batch, seq)

        h = _outproj_ln(ret, ssd, h, mix_w_out[layer].astype(BF16), row(ln2_gain[layer]),
                        row(ln2_bias[layer]), alpha)

        h, _ = _ffn_ln(h, *ffn2_w, row(ln3_gain[layer]), row(ln3_bias[layer]), alpha)

    return h.reshape(batch, seq, d)
```
